```python
import jax, jax.numpy as jnp
from jax import lax
import numpy as np

D_MODEL = 1024
BATCH = 8
SEQ = 4096
DEPTH = 2

GRID_W = 64
CTX_LEN = 256
HEAD_DIM = 64
NA_WIDTH = D_MODEL // 2
NA_HEADS = NA_WIDTH // HEAD_DIM
NA_ROWS_MAX = 8
NA_COLS = 16
NA_QB = 16
NA_KB = NA_QB + NA_COLS
RET_V_WIDTH = D_MODEL // 2
RET_HEADS = 4
RET_DV = RET_V_WIDTH // RET_HEADS
RET_DK = RET_DV // 2
RET_QK_WIDTH = RET_HEADS * RET_DK
RET_CHUNK = 128
RET_LOG2_DECAY_MIN = -5.0
RET_LOG2_DECAY_MAX = -12.0
POOL_WIDTH = D_MODEL // 2
POOL_WINDOWS = (2, 4, 8, 16)
POOL_GROUPS = 4
POOL_GROUP_WIDTH = POOL_WIDTH // POOL_GROUPS
FFN_HIDDEN = -(-8 * D_MODEL // (3 * 256)) * 256
N_MOD = 6
ROPE_BASE = 10000.0
NORM_EPS = 1e-6
GN_EPS = 1e-5
NEG_INF = -1e30
IN_WIDTHS = (NA_WIDTH, NA_WIDTH, NA_WIDTH, RET_QK_WIDTH, RET_QK_WIDTH, RET_V_WIDTH, RET_V_WIDTH,
             POOL_WIDTH, D_MODEL, D_MODEL, D_MODEL)
IN_WIDTH = 3 * NA_WIDTH + 2 * RET_QK_WIDTH + 2 * RET_V_WIDTH + POOL_WIDTH + 3 * D_MODEL

kernel_name = "hybrid_na_retention_pool_dit"


def _rms_norm(x, g):
    xf = x.astype(jnp.float32)
    y = xf * lax.rsqrt(jnp.mean(xf * xf, axis=-1, keepdims=True) + NORM_EPS)
    return (y * g.astype(jnp.float32)).astype(x.dtype)


def _modulate(h, shift, scale):
    return h * (1.0 + scale) + shift


def _split_cols(u):
    out, start = [], 0
    for w in IN_WIDTHS:
        out.append(u[..., start:start + w])
        start += w
    return out


def _heads(t, n_heads, d):
    b, n, _ = t.shape
    return t.reshape(b, n, n_heads, d).transpose(0, 2, 1, 3)


def _merge_heads(t):
    b, h, n, d = t.shape
    return t.transpose(0, 2, 1, 3).reshape(b, n, h * d)


def _rope_1d(x, pos):
    nf = x.shape[-1] // 2
    inv = ROPE_BASE ** (-jnp.arange(nf, dtype=jnp.float32) / nf)
    ang = pos[:, None] * inv[None, :]
    cos, sin = jnp.cos(ang), jnp.sin(ang)
    x1, x2 = x[..., :nf], x[..., nf:]
    return jnp.concatenate([x1 * cos - x2 * sin, x1 * sin + x2 * cos], axis=-1)


def _axial_rope(x, row, col):
    half = x.shape[-1] // 2
    return jnp.concatenate([_rope_1d(x[..., :half], row), _rope_1d(x[..., half:], col)], axis=-1)


def _dense_attn(q, k, v):
    s = jnp.einsum('bhqd,bhkd->bhqk', q * HEAD_DIM ** -0.5, k).astype(jnp.float32)
    p = jax.nn.softmax(s, axis=-1).astype(v.dtype)
    return jnp.einsum('bhqk,bhkd->bhqd', p, v)


def _na_latent(q, k, v, k_ctx, v_ctx, rpb):
    b, n, _ = q.shape
    rows = n // GRID_W
    kr = min(NA_ROWS_MAX, rows)
    ncb = GRID_W // NA_QB

    def grid(t):
        return t.reshape(b, rows, GRID_W, NA_HEADS, HEAD_DIM).transpose(1, 0, 3, 2, 4)

    qg, kg, vg = grid(q * HEAD_DIM ** -0.5), grid(k), grid(v)
    qcol = np.arange(GRID_W).reshape(ncb, NA_QB)
    kstart = np.clip(qcol[:, 0] - NA_COLS // 2, 0, GRID_W - NA_KB)
    kcol = kstart[:, None] + np.arange(NA_KB)
    wstart = np.clip(qcol - NA_COLS // 2, 0, GRID_W - NA_COLS)
    col_ok = (kcol[:, None, :] >= wstart[..., None]) & (kcol[:, None, :] < wstart[..., None] + NA_COLS)
    dcol_idx = np.clip(kcol[:, None, :] - qcol[:, :, None] + NA_COLS - 1, 0, 2 * NA_COLS - 2)
    rpb_f = rpb.astype(jnp.float32)
    nw = kr * NA_KB

    def row_block(args):
        r, q_r = args
        r0 = jnp.clip(r - kr // 2, 0, rows - kr)
        k_r = lax.dynamic_slice_in_dim(kg, r0, kr, axis=0)
        v_r = lax.dynamic_slice_in_dim(vg, r0, kr, axis=0)
        kb = k_r[:, :, :, kcol].transpose(1, 2, 3, 0, 4, 5).reshape(b, NA_HEADS, ncb, nw, HEAD_DIM)
        vb = v_r[:, :, :, kcol].transpose(1, 2, 3, 0, 4, 5).reshape(b, NA_HEADS, ncb, nw, HEAD_DIM)
        qb = q_r.reshape(b, NA_HEADS, ncb, NA_QB, HEAD_DIM)
        s_win = jnp.einsum('bhnqd,bhnkd->bhnqk', qb, kb).astype(jnp.float32)
        s_win = s_win.reshape(b, NA_HEADS, ncb, NA_QB, kr, NA_KB)
        dr_idx = r0 + jnp.arange(kr) - r + NA_ROWS_MAX - 1
        bias = rpb_f[:, dr_idx][:, :, dcol_idx].transpose(0, 2, 3, 1, 4)
        s_win = jnp.where(col_ok[:, :, None, :], s_win + bias, NEG_INF).reshape(b, NA_HEADS, ncb, NA_QB, nw)
        s_ctx = jnp.einsum('bhnqd,bhkd->bhnqk', qb, k_ctx).astype(jnp.float32)
        p = jax.nn.softmax(jnp.concatenate([s_win, s_ctx], axis=-1), axis=-1).astype(v.dtype)
        o = (jnp.einsum('bhnqk,bhnkd->bhnqd', p[..., :nw], vb)
             + jnp.einsum('bhnqk,bhkd->bhnqd', p[..., nw:], v_ctx))
        return o.reshape(b, NA_HEADS, GRID_W, HEAD_DIM)

    out = lax.map(row_block, (jnp.arange(rows), qg))
    return out.transpose(1, 0, 3, 2, 4).reshape(b, n, NA_WIDTH)


def _retention_scan(q, k, v, log_g, state0, include_diag):
    b, h, n, dk = q.shape
    dv = v.shape[-1]
    nc = n // RET_CHUNK
    qc = q.reshape(b, h, nc, RET_CHUNK, dk)
    kc = k.reshape(b, h, nc, RET_CHUNK, dk)
    vc = v.reshape(b, h, nc, RET_CHUNK, dv)
    pos = jnp.arange(RET_CHUNK, dtype=jnp.float32)
    lg = log_g[:, None, None]
    diff = pos[:, None] - pos[None, :]
    mask = (diff >= 0) if include_diag else (diff > 0)
    decay = jnp.where(mask, jnp.exp(lg * jnp.where(mask, diff, 0.0)), 0.0)
    inner = jnp.einsum('bhnqd,bhnkd->bhnqk', qc, kc) * decay[:, None]
    o_in = jnp.einsum('bhnqk,bhnkv->bhnqv', inner, vc)
    zeta = jnp.exp(lg * (RET_CHUNK - 1.0 - pos))[..., None]
    contrib = jnp.einsum('bhnkd,bhnkv->nbhdv', kc * zeta, vc)
    chunk_decay = jnp.exp(log_g * RET_CHUNK)[:, None, None]

    def step(state, ctb):
        return state * chunk_decay + ctb, state

    final, prev = lax.scan(step, state0, contrib)
    xi = jnp.exp(lg * (pos + 1.0))[..., None]
    o_cross = jnp.einsum('bhnqd,nbhdv->bhnqv', qc, prev) * xi
    return (o_in + o_cross).reshape(b, h, n, dv), final


def _bi_retention(q, k, v, lg_f, lg_b, s0_f, s0_b):
    o_f, s_f = _retention_scan(q, k, v, lg_f, s0_f, True)
    flip = lambda t: t[:, :, ::-1]
    o_b, s_b = _retention_scan(flip(q), flip(k), flip(v), lg_b, s0_b, False)
    return o_f + flip(o_b), s_f, s_b


def _ret_out(o, g, gn_g):
    mu = jnp.mean(o, axis=-1, keepdims=True)
    var = jnp.mean(jnp.square(o - mu), axis=-1, keepdims=True)
    y = _merge_heads((o - mu) * lax.rsqrt(var + GN_EPS)) * gn_g.astype(jnp.float32)
    return (jax.nn.silu(g.astype(jnp.float32)) * y).astype(g.dtype)


def _pool_mixer(u, pool_w, pool_scale):
    b, n, _ = u.shape
    uf = u.astype(jnp.float32)
    cs = jnp.concatenate([jnp.zeros((b, 1, POOL_WIDTH), jnp.float32), jnp.cumsum(uf, axis=1)], axis=1)
    t = np.arange(n)
    groups = []
    for gi, w in enumerate(POOL_WINDOWS):
        lo = np.clip(t - w // 2, 0, n)
        hi = np.clip(t - w // 2 + w, 0, n)
        sl = slice(gi * POOL_GROUP_WIDTH, (gi + 1) * POOL_GROUP_WIDTH)
        csg = cs[..., sl]
        mean = (csg[:, hi] - csg[:, lo]) / jnp.asarray(hi - lo, jnp.float32)[None, :, None]
        groups.append(mean - uf[..., sl])
    pooled = jnp.stack(groups, axis=2).astype(u.dtype)
    y = jnp.einsum('bngc,gcd->bngd', pooled, pool_w).reshape(b, n, POOL_WIDTH)
    return y * pool_scale


def _merge(a, r, p, ga, gb, gc, wpa, wpb, wpc, wo):
    m = (jax.nn.sigmoid(ga) * (a @ wpa) + jax.nn.sigmoid(gb) * (r @ wpb)
         + jax.nn.sigmoid(gc) * (p @ wpc))
    return m @ wo


def _swiglu(h, wg, wu, wd):
    return (jax.nn.silu(h @ wg) * (h @ wu)) @ wd


def _token_mixers(hx, hc, w_in, rpb, logit_f, logit_b, gn_g, pool_w, pool_scale,
                  wpa, wpb, wpc, wo, with_ctx):
    b, n, _ = hx.shape
    ux = _split_cols(hx @ w_in)
    uc = _split_cols(hc @ w_in)
    f32 = jnp.float32
    k_na_c = _heads(uc[1], NA_HEADS, HEAD_DIM)
    v_na_c = _heads(uc[2], NA_HEADS, HEAD_DIM)
    a_x = _na_latent(ux[0], ux[1], ux[2], k_na_c, v_na_c, rpb)
    lg_f = jax.nn.log_sigmoid(logit_f.astype(f32))
    lg_b = jax.nn.log_sigmoid(logit_b.astype(f32))
    ksc = RET_DK ** -0.5
    q_rc = _heads(uc[3], RET_HEADS, RET_DK).astype(f32)
    k_rc = _heads(uc[4], RET_HEADS, RET_DK).astype(f32) * ksc
    v_rc = _heads(uc[5], RET_HEADS, RET_DV).astype(f32)
    zero = jnp.zeros((b, RET_HEADS, RET_DK, RET_DV), f32)
    o_rc, s_f, s_b = _bi_retention(q_rc, k_rc, v_rc, lg_f, lg_b, zero, zero)
    tpos = jnp.arange(n)
    row = (tpos // GRID_W).astype(f32)
    col = (tpos % GRID_W).astype(f32)
    q_rx = _axial_rope(_heads(ux[3], RET_HEADS, RET_DK).astype(f32), row, col)
    k_rx = _axial_rope(_heads(ux[4], RET_HEADS, RET_DK).astype(f32), row, col) * ksc
    v_rx = _heads(ux[5], RET_HEADS, RET_DV).astype(f32)
    o_rx, _, _ = _bi_retention(q_rx, k_rx, v_rx, lg_f, lg_b, s_f, s_b)
    r_x = _ret_out(o_rx, ux[6], gn_g)
    p_x = _pool_mixer(ux[7], pool_w, pool_scale)
    mix_x = _merge(a_x, r_x, p_x, ux[8], ux[9], ux[10], wpa, wpb, wpc, wo)
    if not with_ctx:
        return mix_x, None
    a_c = _merge_heads(_dense_attn(_heads(uc[0], NA_HEADS, HEAD_DIM), k_na_c, v_na_c))
    r_c = _ret_out(o_rc, uc[6], gn_g)
    p_c = _pool_mixer(uc[7], pool_w, pool_scale)
    mix_c = _merge(a_c, r_c, p_c, uc[8], uc[9], uc[10], wpa, wpb, wpc, wo)
    return mix_x, mix_c


def setup_inputs(seed: int = 0) -> dict:
    key = jax.random.key(seed)
    ks = jax.random.split(key, 24)
    f32 = jnp.float32
    nrm = lambda k, shape, s: jax.random.normal(k, shape, f32) * s
    g0 = 1.0 - 2.0 ** np.linspace(RET_LOG2_DECAY_MIN, RET_LOG2_DECAY_MAX, RET_HEADS)
    logit0 = jnp.asarray(np.log(g0) - np.log1p(-g0), f32)
    L = DEPTH
    return {
        "x": nrm(ks[0], (BATCH, SEQ, D_MODEL), 1.0),
        "c": nrm(ks[1], (BATCH, D_MODEL), 1.0),
        "ctx": nrm(ks[2], (BATCH, CTX_LEN, D_MODEL), 1.0),
        "c_ctx": nrm(ks[3], (D_MODEL,), 1.0),
        "norm1_g": 1.0 + nrm(ks[4], (L, D_MODEL), 0.02),
        "norm2_g": 1.0 + nrm(ks[5], (L, D_MODEL), 0.02),
        "w_ada": nrm(ks[6], (L, D_MODEL, N_MOD * D_MODEL), 0.5 * D_MODEL ** -0.5),
        "b_ada": nrm(ks[7], (L, N_MOD * D_MODEL), 0.02),
        "w_in": nrm(ks[8], (L, D_MODEL, IN_WIDTH), D_MODEL ** -0.5),
        "na_rpb": nrm(ks[9], (L, NA_HEADS, 2 * NA_ROWS_MAX - 1, 2 * NA_COLS - 1), 0.05),
        "ret_logit_f": logit0 + nrm(ks[10], (L, RET_HEADS), 0.1),
        "ret_logit_b": logit0 + nrm(ks[11], (L, RET_HEADS), 0.1),
        "ret_gn_g": 1.0 + nrm(ks[12], (L, RET_V_WIDTH), 0.02),
        "pool_w": nrm(ks[13], (L, POOL_GROUPS, POOL_GROUP_WIDTH, POOL_GROUP_WIDTH), POOL_GROUP_WIDTH ** -0.5),
        "pool_scale": 1.0 + nrm(ks[14], (L, POOL_WIDTH), 0.02),
        "w_branch_a": nrm(ks[15], (L, NA_WIDTH, D_MODEL), NA_WIDTH ** -0.5),
        "w_branch_b": nrm(ks[16], (L, RET_V_WIDTH, D_MODEL), RET_V_WIDTH ** -0.5),
        "w_branch_c": nrm(ks[17], (L, POOL_WIDTH, D_MODEL), POOL_WIDTH ** -0.5),
        "w_out": nrm(ks[18], (L, D_MODEL, D_MODEL), D_MODEL ** -0.5),
        "w_ffn_gate": nrm(ks[19], (L, D_MODEL, FFN_HIDDEN), D_MODEL ** -0.5),
        "w_ffn_up": nrm(ks[20], (L, D_MODEL, FFN_HIDDEN), D_MODEL ** -0.5),
        "w_ffn_down": nrm(ks[21], (L, FFN_HIDDEN, D_MODEL), FFN_HIDDEN ** -0.5),
        "final_norm_g": 1.0 + nrm(ks[22], (D_MODEL,), 0.02),
    }


def reference(x, c, ctx, c_ctx, norm1_g, norm2_g, w_ada, b_ada, w_in, na_rpb, ret_logit_f,
              ret_logit_b, ret_gn_g, pool_w, pool_scale, w_branch_a, w_branch_b, w_branch_c,
              w_out, w_ffn_gate, w_ffn_up, w_ffn_down, final_norm_g):
    b = x.shape[0]
    h, hc = x, ctx
    silu_c = jax.nn.silu(c)
    silu_cc = jax.nn.silu(c_ctx)
    for l in range(DEPTH):
        with_ctx = l < DEPTH - 1
        mx = (silu_c @ w_ada[l] + b_ada[l]).reshape(b, N_MOD, 1, D_MODEL)
        mc = (silu_cc @ w_ada[l] + b_ada[l]).reshape(N_MOD, 1, 1, D_MODEL)
        hx_n = _modulate(_rms_norm(h, norm1_g[l]), mx[:, 0], mx[:, 1])
        hc_n = _modulate(_rms_norm(hc, norm1_g[l]), mc[0], mc[1])
        mix_x, mix_c = _token_mixers(hx_n, hc_n, w_in[l], na_rpb[l], ret_logit_f[l], ret_logit_b[l],
                                     ret_gn_g[l], pool_w[l], pool_scale[l], w_branch_a[l],
                                     w_branch_b[l], w_branch_c[l], w_out[l], with_ctx)
        h = h + mx[:, 2] * mix_x
        f_x = _modulate(_rms_norm(h, norm2_g[l]), mx[:, 3], mx[:, 4])
        h = h + mx[:, 5] * _swiglu(f_x, w_ffn_gate[l], w_ffn_up[l], w_ffn_down[l])
        if with_ctx:
            hc = hc + mc[2] * mix_c
            f_c = _modulate(_rms_norm(hc, norm2_g[l]), mc[3], mc[4])
            hc = hc + mc[5] * _swiglu(f_c, w_ffn_gate[l], w_ffn_up[l], w_ffn_down[l])
    return _rms_norm(h, final_norm_g)
```

```python
import functools

import numpy as np
import jax
import jax.numpy as jnp
from jax import lax
from jax.experimental import pallas as pl
from jax.experimental.pallas import tpu as pltpu

F32 = jnp.float32
BF16 = jnp.bfloat16

LANES = 128
GRID_W = 64
HEAD_DIM = 64
NA_HEADS = 8
NA_ROWS = 8
NA_COLS = 16
RET_HEADS = 4
RET_DK = 64
RET_DV = 128
RET_CHUNK = 128
POOL_WINDOWS = (2, 4, 8, 16)
POOL_HALO = 8
N_MOD = 6
ROPE_BASE = 10000.0
NORM_EPS = 1e-6
GN_EPS = 1e-5
NEG_INF = -1e30
VMEM_LIMIT = 56 * 1024 * 1024

NT_DIMS = (((1,), (1,)), ((), ()))
TN_DIMS = (((0,), (0,)), ((), ()))


def _params(*semantics):
    return pltpu.CompilerParams(dimension_semantics=semantics, vmem_limit_bytes=VMEM_LIMIT)


def _silu(x):
    return x * jax.nn.sigmoid(x)


def _norm_mod(x, g, shift, scale):
    ms = jnp.mean(x * x, axis=-1, keepdims=True)
    y = x * lax.rsqrt(ms + NORM_EPS) * g
    return y * (1.0 + scale) + shift


def _ada_kernel(c_ref, w_ref, b_ref, o_ref):
    s = _silu(c_ref[...])
    o_ref[...] = jnp.dot(s, w_ref[...], preferred_element_type=F32,
                         precision=lax.Precision.HIGHEST) + b_ref[...]


def _ada_call(cvec, w_ada, b_ada):
    depth, d, width = w_ada.shape
    rows = cvec.shape[0]
    tile = width // 4
    return pl.pallas_call(
        _ada_kernel,
        grid=(depth, width // tile),
        in_specs=[pl.BlockSpec((rows, d), lambda l, j: (0, 0)),
                  pl.BlockSpec((None, d, tile), lambda l, j: (l, 0, j)),
                  pl.BlockSpec((None, 1, tile), lambda l, j: (l, 0, j))],
        out_specs=pl.BlockSpec((None, rows, tile), lambda l, j: (l, 0, j)),
        out_shape=jax.ShapeDtypeStruct((depth, rows, width), F32),
        compiler_params=_params("arbitrary", "arbitrary"),
        name="ada",
    )(cvec, w_ada, b_ada)


PROJ_PIECES = (
    ("na_q", 0, 512, HEAD_DIM ** -0.5, BF16),
    ("na_k", 512, 512, 1.0, BF16),
    ("na_v", 1024, 512, 1.0, BF16),
    ("ret_q", 1536, 256, 1.0, F32),
    ("ret_k", 1792, 256, 1.0, F32),
    ("ret_v", 2048, 512, 1.0, BF16),
    ("ret_g", 2560, 512, 1.0, F32),
    ("pool", 3072, 512, 1.0, F32),
)
PROJ_WIDTH = 3584
GATE_WIDTH = 3072


def _proj_kernel(h_ref, mod_ref, g_ref, w_ref, *o_refs):
    hn = _norm_mod(h_ref[...], g_ref[...], mod_ref[0:1, :], mod_ref[1:2, :]).astype(BF16)
    for o_ref, (_, start, width, scale, _) in zip(o_refs, PROJ_PIECES):
        acc = jnp.dot(hn, w_ref[:, start:start + width], preferred_element_type=F32)
        if scale != 1.0:
            acc = acc * scale
        o_ref[...] = acc.astype(o_ref.dtype)


def _mod_spec(mod):
    d = mod.shape[-1]
    if mod.shape[0] == 1:
        return pl.BlockSpec((None, 8, d), lambda b, i: (0, 0, 0))
    return pl.BlockSpec((None, 8, d), lambda b, i: (b, 0, 0))


def _const_spec(shape):
    zeros = (0,) * len(shape)
    return pl.BlockSpec(shape, lambda b, i: zeros)


def _proj_call(h, mod, g, w, block):
    bsz, n, d = h.shape
    tok = lambda width: pl.BlockSpec((None, block, width), lambda b, i: (b, i, 0))
    return pl.pallas_call(
        _proj_kernel,
        grid=(bsz, n // block),
        in_specs=[tok(d), _mod_spec(mod), _const_spec((1, d)), _const_spec(w.shape)],
        out_specs=[tok(p[2]) for p in PROJ_PIECES],
        out_shape=[jax.ShapeDtypeStruct((bsz, n, p[2]), p[4]) for p in PROJ_PIECES],
        compiler_params=_params("arbitrary", "arbitrary"),
        name="proj",
    )(h, mod, g, w)


def _pair_softmax_pv(s_list, v_list):
    m = functools.reduce(jnp.maximum, [jnp.max(s, axis=1, keepdims=True) for s in s_list])
    p_list = [jnp.exp(s - m) for s in s_list]
    l = functools.reduce(jnp.add, [jnp.sum(p, axis=1, keepdims=True) for p in p_list])
    o = functools.reduce(jnp.add, [jnp.dot(p.astype(BF16), v, preferred_element_type=F32)
                                   for p, v in zip(p_list, v_list)])
    return o / l


def _stack_heads(q, first):
    zero = jnp.zeros_like(q)
    return jnp.concatenate([jnp.where(first, q, zero), jnp.where(first, zero, q)], axis=0)


def _na_kernel(q_ref, k_ref, v_ref, kc_ref, vc_ref, tab_ref, o_ref, *, rows):
    first = lax.broadcasted_iota(jnp.int32, (GRID_W, LANES), 1) < HEAD_DIM
    kc = kc_ref[...]
    vc = vc_ref[...]

    def row_body(r, carry):
        r0 = jnp.clip(r - NA_ROWS // 2, 0, rows - NA_ROWS)
        delta = r - r0
        qoff = pl.multiple_of(r * GRID_W, GRID_W)
        koff = pl.multiple_of(r0 * GRID_W, GRID_W)
        q2 = _stack_heads(q_ref[pl.ds(qoff, GRID_W), :], first)
        kw = k_ref[pl.ds(koff, NA_ROWS * GRID_W), :]
        vw = v_ref[pl.ds(koff, NA_ROWS * GRID_W), :]
        s_win = lax.dot_general(q2, kw, NT_DIMS, preferred_element_type=F32)
        bias = jnp.concatenate(
            [tab_ref[2 * m - delta + (NA_ROWS - 1)] for m in range(NA_ROWS // 2)], axis=1)
        s_ctx = lax.dot_general(q2, kc, NT_DIMS, preferred_element_type=F32)
        o = _pair_softmax_pv([s_win + bias, s_ctx], [vw, vc])
        o_ref[pl.ds(qoff, GRID_W), :] = jnp.where(first, o[:GRID_W], o[GRID_W:]).astype(o_ref.dtype)
        return carry

    lax.fori_loop(0, rows, row_body, 0)


def _na_bias_table(rpb):
    qc = np.arange(GRID_W)[:, None]
    kc = np.arange(GRID_W)[None, :]
    wstart = np.clip(qc - NA_COLS // 2, 0, GRID_W - NA_COLS)
    ok = (kc >= wstart) & (kc < wstart + NA_COLS)
    dcol = np.clip(kc - qc + NA_COLS - 1, 0, 2 * NA_COLS - 2)
    full = jnp.where(ok[None, None], rpb.astype(F32)[:, :, dcol], NEG_INF)
    two = jnp.concatenate([full[:, :-1], full[:, 1:]], axis=-1)
    h, nd = two.shape[0], two.shape[1]
    return two.reshape(h // 2, 2, nd, GRID_W, LANES).transpose(0, 2, 1, 3, 4).reshape(
        h // 2, nd, 2 * GRID_W, LANES)


def _na_call(q, k, v, kc, vc, tab):
    bsz, n, width = q.shape
    nctx = kc.shape[1]
    pairs = width // LANES
    seq = lambda length: pl.BlockSpec((None, length, LANES), lambda b, p: (b, 0, p))
    return pl.pallas_call(
        functools.partial(_na_kernel, rows=n // GRID_W),
        grid=(bsz, pairs),
        in_specs=[seq(n), seq(n), seq(n), seq(nctx), seq(nctx),
                  pl.BlockSpec((None,) + tab.shape[1:], lambda b, p: (p, 0, 0, 0))],
        out_specs=seq(n),
        out_shape=jax.ShapeDtypeStruct((bsz, n, width), BF16),
        compiler_params=_params("arbitrary", "arbitrary"),
        name="na",
    )(q, k, v, kc, vc, tab)


def _ctx_attn_kernel(q_ref, k_ref, v_ref, o_ref):
    n = q_ref.shape[0]
    first = lax.broadcasted_iota(jnp.int32, (n, LANES), 1) < HEAD_DIM
    q2 = _stack_heads(q_ref[...], first)
    s = lax.dot_general(q2, k_ref[...], NT_DIMS, preferred_element_type=F32)
    o = _pair_softmax_pv([s], [v_ref[...]])
    o_ref[...] = jnp.where(first, o[:n], o[n:]).astype(o_ref.dtype)


def _ctx_attn_call(q, k, v):
    bsz, n, width = q.shape
    seq = pl.BlockSpec((None, n, LANES), lambda b, p: (b, 0, p))
    return pl.pallas_call(
        _ctx_attn_kernel,
        grid=(bsz, width // LANES),
        in_specs=[seq, seq, seq],
        out_specs=seq,
        out_shape=jax.ShapeDtypeStruct((bsz, n, width), BF16),
        compiler_params=_params("arbitrary", "arbitrary"),
        name="ctx_attn",
    )(q, k, v)


C = RET_CHUNK
T_DEC = 0
T_XIF = 4
T_XIB = 8
T_ZF = 12
T_ZB = 14
T_CF = 16
T_CB = 18
T_COUNT = 20


def _ret_tables(lg_ref, tabs_ref):
    row = lax.broadcasted_iota(jnp.int32, (C, C), 0)
    col = lax.broadcasted_iota(jnp.int32, (C, C), 1)
    rowf = row.astype(F32)
    diff = (row - col).astype(F32)
    for h in range(RET_HEADS):
        lf = lg_ref[0, h]
        lb = lg_ref[1, h]
        tabs_ref[T_DEC + h] = jnp.where(diff >= 0, jnp.exp(lf * jnp.maximum(diff, 0.0)),
                                        jnp.exp(lb * jnp.maximum(-diff, 0.0)))
        tabs_ref[T_XIF + h] = jnp.exp(lf * (rowf + 1.0))
        tabs_ref[T_XIB + h] = jnp.exp(lb * (C - rowf))
    for p in range(RET_HEADS // 2):
        lf_lane = jnp.where(col < RET_DK, lg_ref[0, 2 * p], lg_ref[0, 2 * p + 1])
        lb_lane = jnp.where(col < RET_DK, lg_ref[1, 2 * p], lg_ref[1, 2 * p + 1])
        lf_row = jnp.where(row < RET_DK, lg_ref[0, 2 * p], lg_ref[0, 2 * p + 1])
        lb_row = jnp.where(row < RET_DK, lg_ref[1, 2 * p], lg_ref[1, 2 * p + 1])
        tabs_ref[T_ZF + p] = jnp.exp(lf_lane * (C - 1.0 - rowf))
        tabs_ref[T_ZB + p] = jnp.exp(lb_lane * rowf)
        tabs_ref[T_CF + p] = jnp.exp(lf_row * float(C))
        tabs_ref[T_CB + p] = jnp.exp(lb_row * float(C))


def _rope(x, cos, sin):
    lane = lax.broadcasted_iota(jnp.int32, x.shape, 1)
    low = (lane % 32) < 16
    partner = jnp.where(low, pltpu.roll(x, LANES - 16, 1), pltpu.roll(x, 16, 1))
    return x * cos + partner * sin


def _rope_tables(n):
    nf = RET_DK // 4
    inv = ROPE_BASE ** (-np.arange(nf, dtype=np.float64) / nf)
    t = np.arange(n)
    row = (t // GRID_W).astype(np.float64)
    col = (t % GRID_W).astype(np.float64)
    ang_row = row[:, None] * inv[None, :]
    ang_col = col[:, None] * inv[None, :]
    ang = np.concatenate([ang_row, ang_row, ang_col, ang_col], axis=1)
    sign = np.concatenate([-np.ones(nf), np.ones(nf)] * 2)[None, :]
    cos = np.tile(np.cos(ang), (1, 2))
    sin = np.tile(np.sin(ang) * sign, (1, 2))
    return jnp.asarray(cos, F32), jnp.asarray(sin, F32)


def _state_delta(kz, v2, row_first):
    full = lax.dot_general(kz.astype(BF16), v2, TN_DIMS, preferred_element_type=F32)
    return jnp.where(row_first, full[:, :RET_DV], full[:, RET_DV:])


def _ret_kernel(lg_ref, qx_ref, kx_ref, vx_ref, gx_ref, qc_ref, kc_ref, vc_ref, gc_ref,
                cos_ref, sin_ref, gn_ref, ox_ref, oc_ref,
                tabs_ref, sbs_ref, sb_ref, sf_ref, *, ctx_chunks, x_chunks):
    phase = pl.program_id(1)
    t = pl.program_id(2)
    total = ctx_chunks + x_chunks
    u_back = jnp.where(t < ctx_chunks, ctx_chunks - 1 - t, total + ctx_chunks - 1 - t)
    u = jnp.where(phase == 0, u_back, t)
    lane_first = lax.broadcasted_iota(jnp.int32, (C, LANES), 1) < RET_DK
    row_first = lax.broadcasted_iota(jnp.int32, (C, LANES), 0) < RET_DK
    ksc = RET_DK ** -0.5

    @pl.when((phase == 0) & (t == 0))
    def _():
        _ret_tables(lg_ref, tabs_ref)
        sb_ref[...] = jnp.zeros_like(sb_ref)
        sf_ref[...] = jnp.zeros_like(sf_ref)

    def load_qk(ref, p, cs):
        x = ref[:, p * LANES:(p + 1) * LANES]
        if cs is not None:
            x = _rope(x, cs[0], cs[1])
        return x

    def backward_step(k_ref, v_ref, cs):
        sbs_ref[u] = sb_ref[...]
        for p in range(RET_HEADS // 2):
            ks = load_qk(k_ref, p, cs) * ksc
            delta = _state_delta(ks * tabs_ref[T_ZB + p], v_ref[:, 2 * p * RET_DV:(2 * p + 2) * RET_DV],
                                 row_first)
            sb_ref[p] = sb_ref[p] * tabs_ref[T_CB + p] + delta

    def forward_step(q_ref, k_ref, v_ref, g_ref, o_ref, cs):
        for p in range(RET_HEADS // 2):
            q = load_qk(q_ref, p, cs)
            ks = load_qk(k_ref, p, cs) * ksc
            kb = ks.astype(BF16)
            zero = jnp.zeros_like(q)
            q_heads = (jnp.where(lane_first, q, zero).astype(BF16),
                       jnp.where(lane_first, zero, q).astype(BF16))
            sf_b = sf_ref[p].astype(BF16)
            sb_b = sbs_ref[u, p].astype(BF16)
            for j in range(2):
                h = 2 * p + j
                cols = slice(h * RET_DV, (h + 1) * RET_DV)
                qh = q_heads[j]
                s = lax.dot_general(qh, kb, NT_DIMS, preferred_element_type=F32)
                o = jnp.dot((s * tabs_ref[T_DEC + h]).astype(BF16), v_ref[:, cols],
                            preferred_element_type=F32)
                o = o + jnp.dot(qh, sf_b, preferred_element_type=F32) * tabs_ref[T_XIF + h]
                o = o + jnp.dot(qh, sb_b, preferred_element_type=F32) * tabs_ref[T_XIB + h]
                mu = jnp.mean(o, axis=-1, keepdims=True)
                d = o - mu
                var = jnp.mean(d * d, axis=-1, keepdims=True)
                y = d * lax.rsqrt(var + GN_EPS) * gn_ref[:, cols]
                o_ref[:, cols] = (_silu(g_ref[:, cols]) * y).astype(o_ref.dtype)
            delta = _state_delta(ks * tabs_ref[T_ZF + p], v_ref[:, 2 * p * RET_DV:(2 * p + 2) * RET_DV],
                                 row_first)
            sf_ref[p] = sf_ref[p] * tabs_ref[T_CF + p] + delta

    def latent_tables():
        off = pl.multiple_of((u - ctx_chunks) * C, C)
        return cos_ref[pl.ds(off, C), :], sin_ref[pl.ds(off, C), :]

    @pl.when((phase == 0) & (u < ctx_chunks))
    def _():
        backward_step(kc_ref, vc_ref, None)

    @pl.when((phase == 0) & (u >= ctx_chunks))
    def _():
        backward_step(kx_ref, vx_ref, latent_tables())

    @pl.when((phase == 1) & (u < ctx_chunks))
    def _():
        forward_step(qc_ref, kc_ref, vc_ref, gc_ref, oc_ref, None)

    @pl.when((phase == 1) & (u >= ctx_chunks))
    def _():
        forward_step(qx_ref, kx_ref, vx_ref, gx_ref, ox_ref, latent_tables())


def _ret_call(lg, qx, kx, vx, gx, qc, kc, vc, gc, cos, sin, gn):
    bsz, n, _ = qx.shape
    nctx = qc.shape[1]
    ctx_chunks, x_chunks = nctx // C, n // C
    total = ctx_chunks + x_chunks

    def unified(ph, t):
        back = jnp.where(t < ctx_chunks, ctx_chunks - 1 - t, total + ctx_chunks - 1 - t)
        return jnp.where(ph == 0, back, t)

    def ctx_idx(u):
        return jnp.clip(u, 0, ctx_chunks - 1)

    def x_idx(u):
        return jnp.clip(u - ctx_chunks, 0, x_chunks - 1)

    def both(width, pick):
        return pl.BlockSpec((None, C, width), lambda b, ph, t: (b, pick(unified(ph, t)), 0))

    def fwd_only(width, pick):
        return pl.BlockSpec((None, C, width), lambda b, ph, t: (b, pick(ph * t), 0))

    qk_w, v_w = qx.shape[2], vx.shape[2]
    return pl.pallas_call(
        functools.partial(_ret_kernel, ctx_chunks=ctx_chunks, x_chunks=x_chunks),
        grid=(bsz, 2, total),
        in_specs=[pl.BlockSpec(memory_space=pltpu.SMEM),
                  fwd_only(qk_w, x_idx), both(qk_w, x_idx), both(v_w, x_idx), fwd_only(v_w, x_idx),
                  fwd_only(qk_w, ctx_idx), both(qk_w, ctx_idx), both(v_w, ctx_idx), fwd_only(v_w, ctx_idx),
                  pl.BlockSpec(cos.shape, lambda b, ph, t: (0, 0)),
                  pl.BlockSpec(sin.shape, lambda b, ph, t: (0, 0)),
                  pl.BlockSpec(gn.shape, lambda b, ph, t: (0, 0))],
        out_specs=[fwd_only(v_w, x_idx), fwd_only(v_w, ctx_idx)],
        out_shape=[jax.ShapeDtypeStruct((bsz, n, v_w), BF16),
                   jax.ShapeDtypeStruct((bsz, nctx, v_w), BF16)],
        scratch_shapes=[pltpu.VMEM((T_COUNT, C, C), F32),
                        pltpu.VMEM((total, RET_HEADS // 2, 2 * RET_DK, RET_DV), F32),
                        pltpu.VMEM((RET_HEADS // 2, 2 * RET_DK, RET_DV), F32),
                        pltpu.VMEM((RET_HEADS // 2, 2 * RET_DK, RET_DV), F32)],
        compiler_params=_params("arbitrary", "arbitrary", "arbitrary"),
        name="retention",
    )(lg, qx, kx, vx, gx, qc, kc, vc, gc, cos, sin, gn)


def _merge_kernel(h_ref, mod_ref, g_ref, a_ref, r_ref, u_ref, uprev_ref, unext_ref,
                  wg_ref, wpa_ref, wpb_ref, wpc_ref, wo_ref, pw_ref, ps_ref, o_ref, ext_ref, *, seq_len):
    i = pl.program_id(1)
    block = h_ref.shape[0]
    d = h_ref.shape[1]
    h = h_ref[...]
    hn = _norm_mod(h, g_ref[...], mod_ref[0:1, :], mod_ref[1:2, :]).astype(BF16)

    zero_halo = jnp.zeros((POOL_HALO, u_ref.shape[1]), F32)
    ext_ref[0:POOL_HALO, :] = jnp.where(i > 0, uprev_ref[...], zero_halo)
    ext_ref[POOL_HALO:POOL_HALO + block, :] = u_ref[...]
    ext_ref[POOL_HALO + block:, :] = jnp.where(i < pl.num_programs(1) - 1, unext_ref[...], zero_halo)
    tpos = i * block + lax.broadcasted_iota(jnp.int32, (block, LANES), 0)
    pooled_out = []
    for gi, w in enumerate(POOL_WINDOWS):
        lanes = slice(gi * LANES, (gi + 1) * LANES)
        acc = ext_ref[POOL_HALO - w // 2:POOL_HALO - w // 2 + block, lanes]
        for s in range(1 - w // 2, w // 2):
            acc = acc + ext_ref[POOL_HALO + s:POOL_HALO + s + block, lanes]
        cnt = (jnp.minimum(tpos + w // 2, seq_len) - jnp.maximum(tpos - w // 2, 0)).astype(F32)
        pooled = acc / cnt - u_ref[:, lanes]
        pooled_out.append(jnp.dot(pooled.astype(BF16), pw_ref[gi], preferred_element_type=F32))
    pool = (jnp.concatenate(pooled_out, axis=1) * ps_ref[...]).astype(BF16)

    def gated(k, x, w_ref):
        gate = jnp.dot(hn, wg_ref[:, k * d:(k + 1) * d], preferred_element_type=F32)
        return jax.nn.sigmoid(gate) * jnp.dot(x, w_ref[...], preferred_element_type=F32)

    m = gated(0, a_ref[...], wpa_ref) + gated(1, r_ref[...], wpb_ref) + gated(2, pool, wpc_ref)
    mix = jnp.dot(m.astype(BF16), wo_ref[...], preferred_element_type=F32)
    o_ref[...] = h + mod_ref[2:3, :] * mix


def _merge_call(h, mod, g, a, r, u, wg, wpa, wpb, wpc, wo, pw, ps, block):
    bsz, n, d = h.shape
    pw_w = u.shape[2]
    halo_blocks = block // POOL_HALO
    tok = lambda width: pl.BlockSpec((None, block, width), lambda b, i: (b, i, 0))
    prev = pl.BlockSpec((None, POOL_HALO, pw_w),
                        lambda b, i: (b, jnp.maximum(i * halo_blocks - 1, 0), 0))
    nxt = pl.BlockSpec((None, POOL_HALO, pw_w),
                       lambda b, i: (b, jnp.minimum((i + 1) * halo_blocks, n // POOL_HALO - 1), 0))
    return pl.pallas_call(
        functools.partial(_merge_kernel, seq_len=n),
        grid=(bsz, n // block),
        in_specs=[tok(d), _mod_spec(mod), _const_spec((1, d)), tok(a.shape[2]), tok(r.shape[2]),
                  tok(pw_w), prev, nxt,
                  _const_spec(wg.shape), _const_spec(wpa.shape), _const_spec(wpb.shape),
                  _const_spec(wpc.shape), _const_spec(wo.shape), _const_spec(pw.shape),
                  _const_spec(ps.shape)],
        out_specs=tok(d),
        out_shape=jax.ShapeDtypeStruct((bsz, n, d), F32),
        scratch_shapes=[pltpu.VMEM((block + 2 * POOL_HALO, pw_w), F32)],
        compiler_params=_params("arbitrary", "arbitrary"),
        name="merge",
    )(h, mod, g, a, r, u, u, u, wg, wpa, wpb, wpc, wo, pw, ps)


def _ffn_kernel(h_ref, mod_ref, g_ref, wg_ref, wu_ref, wd_ref, fg_ref, o_ref, *, hidden_tile, final):
    h = h_ref[...]
    f = _norm_mod(h, g_ref[...], mod_ref[3:4, :], mod_ref[4:5, :]).astype(BF16)
    hidden = wg_ref.shape[1]
    acc = None
    for start in range(0, hidden, hidden_tile):
        cols = slice(start, start + hidden_tile)
        gate = jnp.dot(f, wg_ref[:, cols], preferred_element_type=F32)
        up = jnp.dot(f, wu_ref[:, cols], preferred_element_type=F32)
        part = jnp.dot((_silu(gate) * up).astype(BF16), wd_ref[cols, :], preferred_element_type=F32)
        acc = part if acc is None else acc + part
    out = h + mod_ref[5:6, :] * acc
    if final:
        ms = jnp.mean(out * out, axis=-1, keepdims=True)
        out = out * lax.rsqrt(ms + NORM_EPS) * fg_ref[...]
    o_ref[...] = out


def _ffn_call(h, mod, g, wg, wu, wd, final_g, block, final):
    bsz, n, d = h.shape
    hidden = wg.shape[1]
    tok = pl.BlockSpec((None, block, d), lambda b, i: (b, i, 0))
    return pl.pallas_call(
        functools.partial(_ffn_kernel, hidden_tile=hidden // 2, final=final),
        grid=(bsz, n // block),
        in_specs=[tok, _mod_spec(mod), _const_spec((1, d)), _const_spec(wg.shape),
                  _const_spec(wu.shape), _const_spec(wd.shape), _const_spec((1, d))],
        out_specs=tok,
        out_shape=jax.ShapeDtypeStruct((bsz, n, d), F32),
        compiler_params=_params("arbitrary", "arbitrary"),
        name="ffn",
    )(h, mod, g, wg, wu, wd, final_g)


X_BLOCK = 256
CTX_BLOCK = 256


def kernel(x, c, ctx, c_ctx, norm1_g, norm2_g, w_ada, b_ada, w_in, na_rpb, ret_logit_f, ret_logit_b,
           ret_gn_g, pool_w, pool_scale, w_branch_a, w_branch_b, w_branch_c, w_out, w_ffn_gate,
           w_ffn_up, w_ffn_down, final_norm_g):
    bsz, n, d = x.shape
    depth = w_in.shape[0]
    assert n % GRID_W == 0 and n // GRID_W >= NA_ROWS and n % X_BLOCK == 0
    assert ctx.shape[1] == CTX_BLOCK and bsz <= 8

    cvec = jnp.zeros((16, d), F32).at[:bsz].set(c).at[8].set(c_ctx)
    ada = _ada_call(cvec, w_ada, b_ada.reshape(depth, 1, -1)).reshape(depth, 16, N_MOD, d)
    ada = jnp.pad(ada, ((0, 0), (0, 0), (0, 8 - N_MOD), (0, 0)))

    w_in_b = w_in.astype(BF16)
    wpa_b, wpb_b, wpc_b = (w.astype(BF16) for w in (w_branch_a, w_branch_b, w_branch_c))
    wo_b, pw_b = w_out.astype(BF16), pool_w.astype(BF16)
    wfg_b, wfu_b, wfd_b = (w.astype(BF16) for w in (w_ffn_gate, w_ffn_up, w_ffn_down))
    lg = jnp.stack([jax.nn.log_sigmoid(ret_logit_f.astype(F32)),
                    jax.nn.log_sigmoid(ret_logit_b.astype(F32))], axis=1)
    cos, sin = _rope_tables(n)
    row = lambda v: v.reshape(1, -1)

    h, hc = x, ctx
    for l in range(depth):
        last = l == depth - 1
        mod_x, mod_c = ada[l, :bsz], ada[l, 8:9]
        w_proj, w_gate = w_in_b[l, :, :PROJ_WIDTH], w_in_b[l, :, PROJ_WIDTH:]
        ux = dict(zip([p[0] for p in PROJ_PIECES],
                      _proj_call(h, mod_x, row(norm1_g[l]), w_proj, X_BLOCK)))
        uc = dict(zip([p[0] for p in PROJ_PIECES],
                      _proj_call(hc, mod_c, row(norm1_g[l]), w_proj, CTX_BLOCK)))
        a_x = _na_call(ux["na_q"], ux["na_k"], ux["na_v"], uc["na_k"], uc["na_v"],
                       _na_bias_table(na_rpb[l]))
        r_x, r_c = _ret_call(lg[l], ux["ret_q"], ux["ret_k"], ux["ret_v"], ux["ret_g"],
                             uc["ret_q"], uc["ret_k"], uc["ret_v"], uc["ret_g"],
                             cos, sin, row(ret_gn_g[l]))
        merge_w = (w_gate, wpa_b[l], wpb_b[l], wpc_b[l], wo_b[l], pw_b[l], row(pool_scale[l]))
        ffn_w = (wfg_b[l], wfu_b[l], wfd_b[l], row(final_norm_g))
        h = _merge_call(h, mod_x, row(norm1_g[l]), a_x, r_x, ux["pool"], *merge_w, X_BLOCK)
        h = _ffn_call(h, mod_x, row(norm2_g[l]), *ffn_w, X_BLOCK, last)
        if not last:
            a_c = _ctx_attn_call(uc["na_q"], uc["na_k"], uc["na_v"])
            hc = _merge_call(hc, mod_c, row(norm1_g[l]), a_c, r_c, uc["pool"], *merge_w, CTX_BLOCK)
            hc = _ffn_call(hc, mod_c, row(norm2_g[l]), *ffn_w, CTX_BLOCK, False)
    return h
```

```python
import functools

import numpy as np
import jax
import jax.numpy as jnp
from jax import lax
from jax.experimental import pallas as pl
from jax.experimental.pallas import tpu as pltpu

F32 = jnp.float32
BF16 = jnp.bfloat16

LANES = 128
GRID_W = 64
HEAD_DIM = 64
NA_HEADS = 8
NA_ROWS = 8
NA_COLS = 16
RET_HEADS = 4
RET_DK = 64
RET_DV = 128
RET_CHUNK = 128
POOL_WINDOWS = (2, 4, 8, 16)
POOL_HALO = 8
N_MOD = 6
ROPE_BASE = 10000.0
NORM_EPS = 1e-6
GN_EPS = 1e-5
NEG_INF = -1e30
VMEM_LIMIT = 56 * 1024 * 1024

NT_DIMS = (((1,), (1,)), ((), ()))
TN_DIMS = (((0,), (0,)), ((), ()))


def _params(*semantics):
    return pltpu.CompilerParams(dimension_semantics=semantics, vmem_limit_bytes=VMEM_LIMIT)


def _silu(x):
    return x * jax.nn.sigmoid(x)


def _norm_mod(x, g, shift, scale):
    ms = jnp.mean(x * x, axis=-1, keepdims=True)
    y = x * lax.rsqrt(ms + NORM_EPS) * g
    return y * (1.0 + scale) + shift


def _ada_kernel(c_ref, w_ref, b_ref, o_ref):
    s = _silu(c_ref[...])
    o_ref[...] = jnp.dot(s, w_ref[...], preferred_element_type=F32,
                         precision=lax.Precision.HIGHEST) + b_ref[...]


def _ada_call(cvec, w_ada, b_ada):
    depth, d, width = w_ada.shape
    rows = cvec.shape[0]
    tile = width // 4
    return pl.pallas_call(
        _ada_kernel,
        grid=(depth, width // tile),
        in_specs=[pl.BlockSpec((rows, d), lambda l, j: (0, 0)),
                  pl.BlockSpec((None, d, tile), lambda l, j: (l, 0, j)),
                  pl.BlockSpec((None, 1, tile), lambda l, j: (l, 0, j))],
        out_specs=pl.BlockSpec((None, rows, tile), lambda l, j: (l, 0, j)),
        out_shape=jax.ShapeDtypeStruct((depth, rows, width), F32),
        compiler_params=_params("arbitrary", "arbitrary"),
        name="ada",
    )(cvec, w_ada, b_ada)


PROJ_PIECES = (
    ("na_q", 0, 512, HEAD_DIM ** -0.5, BF16),
    ("na_k", 512, 512, 1.0, BF16),
    ("na_v", 1024, 512, 1.0, BF16),
    ("ret_q", 1536, 256, 1.0, F32),
    ("ret_k", 1792, 256, 1.0, F32),
    ("ret_v", 2048, 512, 1.0, BF16),
    ("ret_g", 2560, 512, 1.0, F32),
    ("pool", 3072, 512, 1.0, F32),
)
PROJ_WIDTH = 3584
GATE_WIDTH = 3072


def _proj_kernel(h_ref, mod_ref, g_ref, w_ref, *o_refs):
    hn = _norm_mod(h_ref[...], g_ref[...], mod_ref[0:1, :], mod_ref[1:2, :]).astype(BF16)
    for o_ref, (_, start, width, scale, _) in zip(o_refs, PROJ_PIECES):
        acc = jnp.dot(hn, w_ref[:, start:start + width], preferred_element_type=F32)
        if scale != 1.0:
            acc = acc * scale
        o_ref[...] = acc.astype(o_ref.dtype)


def _mod_spec(mod):
    d = mod.shape[-1]
    if mod.shape[0] == 1:
        return pl.BlockSpec((None, 8, d), lambda b, i: (0, 0, 0))
    return pl.BlockSpec((None, 8, d), lambda b, i: (b, 0, 0))


def _const_spec(shape):
    zeros = (0,) * len(shape)
    return pl.BlockSpec(shape, lambda b, i: zeros)


def _proj_call(h, mod, g, w, block):
    bsz, n, d = h.shape
    tok = lambda width: pl.BlockSpec((None, block, width), lambda b, i: (b, i, 0))
    return pl.pallas_call(
        _proj_kernel,
        grid=(bsz, n // block),
        in_specs=[tok(d), _mod_spec(mod), _const_spec((1, d)), _const_spec(w.shape)],
        out_specs=[tok(p[2]) for p in PROJ_PIECES],
        out_shape=[jax.ShapeDtypeStruct((bsz, n, p[2]), p[4]) for p in PROJ_PIECES],
        compiler_params=_params("arbitrary", "arbitrary"),
        name="proj",
    )(h, mod, g, w)


def _pair_softmax_pv(s_list, v_list):
    m = functools.reduce(jnp.maximum, [jnp.max(s, axis=1, keepdims=True) for s in s_list])
    p_list = [jnp.exp(s - m) for s in s_list]
    l = functools.reduce(jnp.add, [jnp.sum(p, axis=1, keepdims=True) for p in p_list])
    o = functools.reduce(jnp.add, [jnp.dot(p.astype(BF16), v, preferred_element_type=F32)
                                   for p, v in zip(p_list, v_list)])
    return o / l


def _stack_heads(q, first):
    zero = jnp.zeros_like(q)
    return jnp.concatenate([jnp.where(first, q, zero), jnp.where(first, zero, q)], axis=0)


def _na_kernel(q_ref, k_ref, v_ref, kc_ref, vc_ref, tab_ref, o_ref,
               vaug_ref, vcaug_ref, s_ref, p_ref, *, rows):
    first = lax.broadcasted_iota(jnp.int32, (GRID_W, LANES), 1) < HEAD_DIM
    win = NA_ROWS * GRID_W
    vaug_ref[:, :LANES] = v_ref[...]
    vaug_ref[:, LANES:] = jnp.ones((v_ref.shape[0], LANES), BF16)
    vcaug_ref[:, :LANES] = vc_ref[...]
    vcaug_ref[:, LANES:] = jnp.ones((vc_ref.shape[0], LANES), BF16)

    def window_start(r):
        return jnp.clip(r - NA_ROWS // 2, 0, rows - NA_ROWS)

    def scores(r, slot):
        r0 = window_start(r)
        delta = r - r0
        q2 = _stack_heads(q_ref[pl.ds(pl.multiple_of(r * GRID_W, GRID_W), GRID_W), :], first)
        kw = k_ref[pl.ds(pl.multiple_of(r0 * GRID_W, GRID_W), win), :]
        s_win = lax.dot_general(q2, kw, NT_DIMS, preferred_element_type=F32)
        for m in range(NA_ROWS // 2):
            cols = slice(m * LANES, (m + 1) * LANES)
            s_ref[slot, :, cols] = s_win[:, cols] + tab_ref[2 * m - delta + (NA_ROWS - 1)]
        s_ref[slot, :, win:] = lax.dot_general(q2, kc_ref[...], NT_DIMS, preferred_element_type=F32)

    def numerators(slot):
        s = s_ref[slot]
        p_ref[slot] = jnp.exp(s - jnp.max(s, axis=1, keepdims=True)).astype(BF16)

    def values(r, slot):
        r0 = window_start(r)
        vw = vaug_ref[pl.ds(pl.multiple_of(r0 * GRID_W, GRID_W), win), :]
        o = (jnp.dot(p_ref[slot, :, :win], vw, preferred_element_type=F32)
             + jnp.dot(p_ref[slot, :, win:], vcaug_ref[...], preferred_element_type=F32))
        o = o[:, :LANES] / o[:, LANES:]
        o_ref[pl.ds(pl.multiple_of(r * GRID_W, GRID_W), GRID_W), :] = jnp.where(
            first, o[:GRID_W], o[GRID_W:]).astype(o_ref.dtype)

    scores(0, 0)
    scores(1, 1)
    numerators(0)

    def pair_body(j, carry):
        for slot in range(2):
            i = 2 * j + slot
            scores(i + 2, slot)
            numerators(1 - slot)
            values(i, slot)
        return carry

    lax.fori_loop(0, rows // 2 - 1, pair_body, 0)
    numerators(1)
    values(rows - 2, 0)
    values(rows - 1, 1)


def _na_bias_table(rpb):
    qc = np.arange(GRID_W)[:, None]
    kc = np.arange(GRID_W)[None, :]
    wstart = np.clip(qc - NA_COLS // 2, 0, GRID_W - NA_COLS)
    ok = (kc >= wstart) & (kc < wstart + NA_COLS)
    dcol = np.clip(kc - qc + NA_COLS - 1, 0, 2 * NA_COLS - 2)
    full = jnp.where(ok[None, None], rpb.astype(F32)[:, :, dcol], NEG_INF)
    two = jnp.concatenate([full[:, :-1], full[:, 1:]], axis=-1)
    h, nd = two.shape[0], two.shape[1]
    return two.reshape(h // 2, 2, nd, GRID_W, LANES).transpose(0, 2, 1, 3, 4).reshape(
        h // 2, nd, 2 * GRID_W, LANES)


def _na_call(q, k, v, kc, vc, tab):
    bsz, n, width = q.shape
    nctx = kc.shape[1]
    pairs = width // LANES
    seq = lambda length: pl.BlockSpec((None, length, LANES), lambda b, p: (b, 0, p))
    return pl.pallas_call(
        functools.partial(_na_kernel, rows=n // GRID_W),
        grid=(bsz, pairs),
        in_specs=[seq(n), seq(n), seq(n), seq(nctx), seq(nctx),
                  pl.BlockSpec((None,) + tab.shape[1:], lambda b, p: (p, 0, 0, 0))],
        out_specs=seq(n),
        out_shape=jax.ShapeDtypeStruct((bsz, n, width), BF16),
        scratch_shapes=[pltpu.VMEM((n, 2 * LANES), BF16),
                        pltpu.VMEM((nctx, 2 * LANES), BF16),
                        pltpu.VMEM((2, 2 * GRID_W, NA_ROWS * GRID_W + nctx), F32),
                        pltpu.VMEM((2, 2 * GRID_W, NA_ROWS * GRID_W + nctx), BF16)],
        compiler_params=_params("arbitrary", "arbitrary"),
        name="na",
    )(q, k, v, kc, vc, tab)


def _ctx_attn_kernel(q_ref, k_ref, v_ref, o_ref):
    n = q_ref.shape[0]
    first = lax.broadcasted_iota(jnp.int32, (n, LANES), 1) < HEAD_DIM
    q2 = _stack_heads(q_ref[...], first)
    s = lax.dot_general(q2, k_ref[...], NT_DIMS, preferred_element_type=F32)
    o = _pair_softmax_pv([s], [v_ref[...]])
    o_ref[...] = jnp.where(first, o[:n], o[n:]).astype(o_ref.dtype)


def _ctx_attn_call(q, k, v):
    bsz, n, width = q.shape
    seq = pl.BlockSpec((None, n, LANES), lambda b, p: (b, 0, p))
    return pl.pallas_call(
        _ctx_attn_kernel,
        grid=(bsz, width // LANES),
        in_specs=[seq, seq, seq],
        out_specs=seq,
        out_shape=jax.ShapeDtypeStruct((bsz, n, width), BF16),
        compiler_params=_params("arbitrary", "arbitrary"),
        name="ctx_attn",
    )(q, k, v)


C = RET_CHUNK
T_DEC = 0
T_XIF = 4
T_XIB = 8
T_ZF = 12
T_ZB = 14
T_CF = 16
T_CB = 18
T_COUNT = 20


def _ret_tables(lg_ref, tabs_ref):
    row = lax.broadcasted_iota(jnp.int32, (C, C), 0)
    col = lax.broadcasted_iota(jnp.int32, (C, C), 1)
    rowf = row.astype(F32)
    diff = (row - col).astype(F32)
    for h in range(RET_HEADS):
        lf = lg_ref[0, h]
        lb = lg_ref[1, h]
        tabs_ref[T_DEC + h] = jnp.where(diff >= 0, jnp.exp(lf * jnp.maximum(diff, 0.0)),
                                        jnp.exp(lb * jnp.maximum(-diff, 0.0)))
        tabs_ref[T_XIF + h] = jnp.exp(lf * (rowf + 1.0))
        tabs_ref[T_XIB + h] = jnp.exp(lb * (C - rowf))
    for p in range(RET_HEADS // 2):
        lf_lane = jnp.where(col < RET_DK, lg_ref[0, 2 * p], lg_ref[0, 2 * p + 1])
        lb_lane = jnp.where(col < RET_DK, lg_ref[1, 2 * p], lg_ref[1, 2 * p + 1])
        lf_row = jnp.where(row < RET_DK, lg_ref[0, 2 * p], lg_ref[0, 2 * p + 1])
        lb_row = jnp.where(row < RET_DK, lg_ref[1, 2 * p], lg_ref[1, 2 * p + 1])
        tabs_ref[T_ZF + p] = jnp.exp(lf_lane * (C - 1.0 - rowf))
        tabs_ref[T_ZB + p] = jnp.exp(lb_lane * rowf)
        tabs_ref[T_CF + p] = jnp.exp(lf_row * float(C))
        tabs_ref[T_CB + p] = jnp.exp(lb_row * float(C))


def _rope(x, cos, sin):
    lane = lax.broadcasted_iota(jnp.int32, x.shape, 1)
    low = (lane % 32) < 16
    partner = jnp.where(low, pltpu.roll(x, LANES - 16, 1), pltpu.roll(x, 16, 1))
    return x * cos + partner * sin


def _rope_tables(n):
    nf = RET_DK // 4
    inv = ROPE_BASE ** (-np.arange(nf, dtype=np.float64) / nf)
    t = np.arange(n)
    row = (t // GRID_W).astype(np.float64)
    col = (t % GRID_W).astype(np.float64)
    ang_row = row[:, None] * inv[None, :]
    ang_col = col[:, None] * inv[None, :]
    ang = np.concatenate([ang_row, ang_row, ang_col, ang_col], axis=1)
    sign = np.concatenate([-np.ones(nf), np.ones(nf)] * 2)[None, :]
    cos = np.tile(np.cos(ang), (1, 2))
    sin = np.tile(np.sin(ang) * sign, (1, 2))
    return jnp.asarray(cos, F32), jnp.asarray(sin, F32)


def _state_delta(kz, v2, row_first):
    full = lax.dot_general(kz.astype(BF16), v2, TN_DIMS, preferred_element_type=F32)
    return jnp.where(row_first, full[:, :RET_DV], full[:, RET_DV:])


def _ret_kernel(lg_ref, qx_ref, kx_ref, vx_ref, gx_ref, qc_ref, kc_ref, vc_ref, gc_ref,
                cos_ref, sin_ref, gn_ref, ox_ref, oc_ref,
                tabs_ref, sbs_ref, sb_ref, sf_ref, *, ctx_chunks, block_chunks, x_blocks):
    phase = pl.program_id(1)
    t = pl.program_id(2)
    blk = jnp.where(phase == 0, x_blocks - t, t - 1)
    u_latent = ctx_chunks + blk * block_chunks
    lane_first = lax.broadcasted_iota(jnp.int32, (C, LANES), 1) < RET_DK
    row_first = lax.broadcasted_iota(jnp.int32, (C, LANES), 0) < RET_DK
    ksc = RET_DK ** -0.5
    pairs = range(RET_HEADS // 2)

    @pl.when((phase == 0) & (t == 0))
    def _():
        _ret_tables(lg_ref, tabs_ref)
        sb_ref[...] = jnp.zeros_like(sb_ref)
        sf_ref[...] = jnp.zeros_like(sf_ref)

    def load_qk(ref, rows, p, rope_off):
        x = ref[rows, p * LANES:(p + 1) * LANES]
        if rope_off is not None:
            off = pl.multiple_of(rope_off + rows.start, C)
            x = _rope(x, cos_ref[pl.ds(off, C), :], sin_ref[pl.ds(off, C), :])
        return x

    def pair_values(v_ref, rows, p):
        return v_ref[rows, 2 * p * RET_DV:(2 * p + 2) * RET_DV]

    def backward_block(k_ref, v_ref, nchunks, u0, rope_off):
        states = [sb_ref[p] for p in pairs]
        for c in reversed(range(nchunks)):
            rows = slice(c * C, (c + 1) * C)
            for p in pairs:
                sbs_ref[u0 + c, p] = states[p]
                ks = load_qk(k_ref, rows, p, rope_off) * ksc
                delta = _state_delta(ks * tabs_ref[T_ZB + p], pair_values(v_ref, rows, p), row_first)
                states[p] = states[p] * tabs_ref[T_CB + p] + delta
        for p in pairs:
            sb_ref[p] = states[p]

    def forward_block(q_ref, k_ref, v_ref, g_ref, o_ref, nchunks, u0, rope_off):
        states = [sf_ref[p] for p in pairs]
        for c in range(nchunks):
            rows = slice(c * C, (c + 1) * C)
            for p in pairs:
                q = load_qk(q_ref, rows, p, rope_off)
                ks = load_qk(k_ref, rows, p, rope_off) * ksc
                kb = ks.astype(BF16)
                zero = jnp.zeros_like(q)
                q_heads = (jnp.where(lane_first, q, zero).astype(BF16),
                           jnp.where(lane_first, zero, q).astype(BF16))
                sf_b = states[p].astype(BF16)
                sb_b = sbs_ref[u0 + c, p].astype(BF16)
                for j in range(2):
                    h = 2 * p + j
                    cols = slice(h * RET_DV, (h + 1) * RET_DV)
                    qh = q_heads[j]
                    s = lax.dot_general(qh, kb, NT_DIMS, preferred_element_type=F32)
                    o = jnp.dot((s * tabs_ref[T_DEC + h]).astype(BF16), v_ref[rows, cols],
                                preferred_element_type=F32)
                    o = o + jnp.dot(qh, sf_b, preferred_element_type=F32) * tabs_ref[T_XIF + h]
                    o = o + jnp.dot(qh, sb_b, preferred_element_type=F32) * tabs_ref[T_XIB + h]
                    mu = jnp.mean(o, axis=-1, keepdims=True)
                    d = o - mu
                    var = jnp.mean(d * d, axis=-1, keepdims=True)
                    y = d * lax.rsqrt(var + GN_EPS) * gn_ref[:, cols]
                    o_ref[rows, cols] = (_silu(g_ref[rows, cols]) * y).astype(o_ref.dtype)
                delta = _state_delta(ks * tabs_ref[T_ZF + p], pair_values(v_ref, rows, p), row_first)
                states[p] = states[p] * tabs_ref[T_CF + p] + delta
        for p in pairs:
            sf_ref[p] = states[p]

    rope_off = blk * (block_chunks * C)

    @pl.when((phase == 0) & (t == 0))
    def _():
        backward_block(kc_ref, vc_ref, ctx_chunks, 0, None)

    @pl.when((phase == 0) & (t > 0))
    def _():
        backward_block(kx_ref, vx_ref, block_chunks, u_latent, rope_off)

    @pl.when((phase == 1) & (t == 0))
    def _():
        forward_block(qc_ref, kc_ref, vc_ref, gc_ref, oc_ref, ctx_chunks, 0, None)

    @pl.when((phase == 1) & (t > 0))
    def _():
        forward_block(qx_ref, kx_ref, vx_ref, gx_ref, ox_ref, block_chunks, u_latent, rope_off)


RET_BLOCK_CHUNKS = 4


def _ret_call(lg, qx, kx, vx, gx, qc, kc, vc, gc, cos, sin, gn):
    bsz, n, _ = qx.shape
    nctx = qc.shape[1]
    ctx_chunks, x_chunks = nctx // C, n // C
    block_chunks = RET_BLOCK_CHUNKS
    x_blocks = x_chunks // block_chunks
    assert x_blocks * block_chunks == x_chunks

    def last_block(i):
        return jnp.clip(i, 0, x_blocks - 1)

    def both(width):
        return pl.BlockSpec((None, block_chunks * C, width),
                            lambda b, ph, t: (b, last_block(jnp.where(ph == 0, x_blocks - t, t - 1)), 0))

    def fwd_only(width):
        return pl.BlockSpec((None, block_chunks * C, width),
                            lambda b, ph, t: (b, last_block(ph * (t - 1)), 0))

    def ctx_all(width):
        return pl.BlockSpec((None, nctx, width), lambda b, ph, t: (b, 0, 0))

    qk_w, v_w = qx.shape[2], vx.shape[2]
    return pl.pallas_call(
        functools.partial(_ret_kernel, ctx_chunks=ctx_chunks, block_chunks=block_chunks,
                          x_blocks=x_blocks),
        grid=(bsz, 2, 1 + x_blocks),
        in_specs=[pl.BlockSpec(memory_space=pltpu.SMEM),
                  fwd_only(qk_w), both(qk_w), both(v_w), fwd_only(v_w),
                  ctx_all(qk_w), ctx_all(qk_w), ctx_all(v_w), ctx_all(v_w),
                  pl.BlockSpec(cos.shape, lambda b, ph, t: (0, 0)),
                  pl.BlockSpec(sin.shape, lambda b, ph, t: (0, 0)),
                  pl.BlockSpec(gn.shape, lambda b, ph, t: (0, 0))],
        out_specs=[fwd_only(v_w), ctx_all(v_w)],
        out_shape=[jax.ShapeDtypeStruct((bsz, n, v_w), BF16),
                   jax.ShapeDtypeStruct((bsz, nctx, v_w), BF16)],
        scratch_shapes=[pltpu.VMEM((T_COUNT, C, C), F32),
                        pltpu.VMEM((ctx_chunks + x_chunks, RET_HEADS // 2, 2 * RET_DK, RET_DV), F32),
                        pltpu.VMEM((RET_HEADS // 2, 2 * RET_DK, RET_DV), F32),
                        pltpu.VMEM((RET_HEADS // 2, 2 * RET_DK, RET_DV), F32)],
        compiler_params=_params("arbitrary", "arbitrary", "arbitrary"),
        name="retention",
    )(lg, qx, kx, vx, gx, qc, kc, vc, gc, cos, sin, gn)


def _merge_kernel(h_ref, mod_ref, g_ref, a_ref, r_ref, u_ref, uprev_ref, unext_ref,
                  wg_ref, wpa_ref, wpb_ref, wpc_ref, wo_ref, pw_ref, ps_ref, o_ref, ext_ref, *, seq_len):
    i = pl.program_id(1)
    block = h_ref.shape[0]
    d = h_ref.shape[1]
    h = h_ref[...]
    hn = _norm_mod(h, g_ref[...], mod_ref[0:1, :], mod_ref[1:2, :]).astype(BF16)

    zero_halo = jnp.zeros((POOL_HALO, u_ref.shape[1]), F32)
    ext_ref[0:POOL_HALO, :] = jnp.where(i > 0, uprev_ref[...], zero_halo)
    ext_ref[POOL_HALO:POOL_HALO + block, :] = u_ref[...]
    ext_ref[POOL_HALO + block:, :] = jnp.where(i < pl.num_programs(1) - 1, unext_ref[...], zero_halo)
    tpos = i * block + lax.broadcasted_iota(jnp.int32, (block, LANES), 0)
    pooled_out = []
    for gi, w in enumerate(POOL_WINDOWS):
        lanes = slice(gi * LANES, (gi + 1) * LANES)
        acc = ext_ref[POOL_HALO - w // 2:POOL_HALO - w // 2 + block, lanes]
        for s in range(1 - w // 2, w // 2):
            acc = acc + ext_ref[POOL_HALO + s:POOL_HALO + s + block, lanes]
        cnt = (jnp.minimum(tpos + w // 2, seq_len) - jnp.maximum(tpos - w // 2, 0)).astype(F32)
        pooled = acc / cnt - u_ref[:, lanes]
        pooled_out.append(jnp.dot(pooled.astype(BF16), pw_ref[gi], preferred_element_type=F32))
    pool = (jnp.concatenate(pooled_out, axis=1) * ps_ref[...]).astype(BF16)

    def gated(k, x, w_ref):
        gate = jnp.dot(hn, wg_ref[:, k * d:(k + 1) * d], preferred_element_type=F32)
        return jax.nn.sigmoid(gate) * jnp.dot(x, w_ref[...], preferred_element_type=F32)

    m = gated(0, a_ref[...], wpa_ref) + gated(1, r_ref[...], wpb_ref) + gated(2, pool, wpc_ref)
    mix = jnp.dot(m.astype(BF16), wo_ref[...], preferred_element_type=F32)
    o_ref[...] = h + mod_ref[2:3, :] * mix


def _merge_call(h, mod, g, a, r, u, wg, wpa, wpb, wpc, wo, pw, ps, block):
    bsz, n, d = h.shape
    pw_w = u.shape[2]
    halo_blocks = block // POOL_HALO
    tok = lambda width: pl.BlockSpec((None, block, width), lambda b, i: (b, i, 0))
    prev = pl.BlockSpec((None, POOL_HALO, pw_w),
                        lambda b, i: (b, jnp.maximum(i * halo_blocks - 1, 0), 0))
    nxt = pl.BlockSpec((None, POOL_HALO, pw_w),
                       lambda b, i: (b, jnp.minimum((i + 1) * halo_blocks, n // POOL_HALO - 1), 0))
    return pl.pallas_call(
        functools.partial(_merge_kernel, seq_len=n),
        grid=(bsz, n // block),
        in_specs=[tok(d), _mod_spec(mod), _const_spec((1, d)), tok(a.shape[2]), tok(r.shape[2]),
                  tok(pw_w), prev, nxt,
                  _const_spec(wg.shape), _const_spec(wpa.shape), _const_spec(wpb.shape),
                  _const_spec(wpc.shape), _const_spec(wo.shape), _const_spec(pw.shape),
                  _const_spec(ps.shape)],
        out_specs=tok(d),
        out_shape=jax.ShapeDtypeStruct((bsz, n, d), F32),
        scratch_shapes=[pltpu.VMEM((block + 2 * POOL_HALO, pw_w), F32)],
        compiler_params=_params("arbitrary", "arbitrary"),
        name="merge",
    )(h, mod, g, a, r, u, u, u, wg, wpa, wpb, wpc, wo, pw, ps)


SUB_ROWS = 512

def _ffn_kernel(h_ref, mod_ref, g_ref, wg_ref, wu_ref, wd_ref, fg_ref, o_ref, *, hidden_tile, final):
    hidden = wg_ref.shape[1]
    for r0 in range(0, h_ref.shape[0], SUB_ROWS):
        rows = slice(r0, min(r0 + SUB_ROWS, h_ref.shape[0]))
        h = h_ref[rows, :]
        f = _norm_mod(h, g_ref[...], mod_ref[3:4, :], mod_ref[4:5, :]).astype(BF16)
        acc = None
        for start in range(0, hidden, hidden_tile):
            cols = slice(start, start + hidden_tile)
            gate = jnp.dot(f, wg_ref[:, cols], preferred_element_type=F32)
            up = jnp.dot(f, wu_ref[:, cols], preferred_element_type=F32)
            part = jnp.dot((_silu(gate) * up).astype(BF16), wd_ref[cols, :],
                           preferred_element_type=F32)
            acc = part if acc is None else acc + part
        out = h + mod_ref[5:6, :] * acc
        if final:
            ms = jnp.mean(out * out, axis=-1, keepdims=True)
            out = out * lax.rsqrt(ms + NORM_EPS) * fg_ref[...]
        o_ref[rows, :] = out


def _ffn_call(h, mod, g, wg, wu, wd, final_g, block, final):
    bsz, n, d = h.shape
    hidden = wg.shape[1]
    tok = pl.BlockSpec((None, block, d), lambda b, i: (b, i, 0))
    return pl.pallas_call(
        functools.partial(_ffn_kernel, hidden_tile=hidden, final=final),
        grid=(bsz, n // block),
        in_specs=[tok, _mod_spec(mod), _const_spec((1, d)), _const_spec(wg.shape),
                  _const_spec(wu.shape), _const_spec(wd.shape), _const_spec((1, d))],
        out_specs=tok,
        out_shape=jax.ShapeDtypeStruct((bsz, n, d), F32),
        compiler_params=_params("arbitrary", "arbitrary"),
        name="ffn",
    )(h, mod, g, wg, wu, wd, final_g)


X_BLOCK = 1024
CTX_BLOCK = 256


def kernel(x, c, ctx, c_ctx, norm1_g, norm2_g, w_ada, b_ada, w_in, na_rpb, ret_logit_f, ret_logit_b,
           ret_gn_g, pool_w, pool_scale, w_branch_a, w_branch_b, w_branch_c, w_out, w_ffn_gate,
           w_ffn_up, w_ffn_down, final_norm_g):
    bsz, n, d = x.shape
    depth = w_in.shape[0]
    assert n % GRID_W == 0 and n // GRID_W >= NA_ROWS and n % X_BLOCK == 0
    assert ctx.shape[1] == CTX_BLOCK and bsz <= 8

    cvec = jnp.zeros((16, d), F32).at[:bsz].set(c).at[8].set(c_ctx)
    ada = _ada_call(cvec, w_ada, b_ada.reshape(depth, 1, -1)).reshape(depth, 16, N_MOD, d)
    ada = jnp.pad(ada, ((0, 0), (0, 0), (0, 8 - N_MOD), (0, 0)))

    w_in_b = w_in.astype(BF16)
    wpa_b, wpb_b, wpc_b = (w.astype(BF16) for w in (w_branch_a, w_branch_b, w_branch_c))
    wo_b, pw_b = w_out.astype(BF16), pool_w.astype(BF16)
    wfg_b, wfu_b, wfd_b = (w.astype(BF16) for w in (w_ffn_gate, w_ffn_up, w_ffn_down))
    lg = jnp.stack([jax.nn.log_sigmoid(ret_logit_f.astype(F32)),
                    jax.nn.log_sigmoid(ret_logit_b.astype(F32))], axis=1)
    cos, sin = _rope_tables(n)
    row = lambda v: v.reshape(1, -1)

    h, hc = x, ctx
    for l in range(depth):
        last = l == depth - 1
        mod_x, mod_c = ada[l, :bsz], ada[l, 8:9]
        w_proj, w_gate = w_in_b[l, :, :PROJ_WIDTH], w_in_b[l, :, PROJ_WIDTH:]
        ux = dict(zip([p[0] for p in PROJ_PIECES],
                      _proj_call(h, mod_x, row(norm1_g[l]), w_proj, X_BLOCK)))
        uc = dict(zip([p[0] for p in PROJ_PIECES],
                      _proj_call(hc, mod_c, row(norm1_g[l]), w_proj, CTX_BLOCK)))
        a_x = _na_call(ux["na_q"], ux["na_k"], ux["na_v"], uc["na_k"], uc["na_v"],
                       _na_bias_table(na_rpb[l]))
        r_x, r_c = _ret_call(lg[l], ux["ret_q"], ux["ret_k"], ux["ret_v"], ux["ret_g"],
                             uc["ret_q"], uc["ret_k"], uc["ret_v"], uc["ret_g"],
                             cos, sin, row(ret_gn_g[l]))
        merge_w = (w_gate, wpa_b[l], wpb_b[l], wpc_b[l], wo_b[l], pw_b[l], row(pool_scale[l]))
        ffn_w = (wfg_b[l], wfu_b[l], wfd_b[l], row(final_norm_g))
        h = _merge_call(h, mod_x, row(norm1_g[l]), a_x, r_x, ux["pool"], *merge_w, X_BLOCK)
        h = _ffn_call(h, mod_x, row(norm2_g[l]), *ffn_w, X_BLOCK, last)
        if not last:
            a_c = _ctx_attn_call(uc["na_q"], uc["na_k"], uc["na_v"])
            hc = _merge_call(hc, mod_c, row(norm1_g[l]), a_c, r_c, uc["pool"], *merge_w, CTX_BLOCK)
            hc = _ffn_call(hc, mod_c, row(norm2_g[l]), *ffn_w, CTX_BLOCK, False)
    return h
```

```python
import functools

import numpy as np
import jax
import jax.numpy as jnp
from jax import lax
from jax.experimental import pallas as pl
from jax.experimental.pallas import tpu as pltpu

F32 = jnp.float32
BF16 = jnp.bfloat16

LANES = 128
GRID_W = 64
HEAD_DIM = 64
NA_HEADS = 8
NA_ROWS = 8
NA_COLS = 16
NA_ROWS_PER_TRIP = 16
RET_HEADS = 4
RET_DK = 64
RET_DV = 128
RET_CHUNK = 128
POOL_WINDOWS = (2, 4, 8, 16)
POOL_HALO = 8
SUB_ROWS = 512
N_MOD = 6
ROPE_BASE = 10000.0
NORM_EPS = 1e-6
GN_EPS = 1e-5
NEG_INF = -1e30
VMEM_LIMIT = 56 * 1024 * 1024

NT_DIMS = (((1,), (1,)), ((), ()))
TN_DIMS = (((0,), (0,)), ((), ()))


def _params(*semantics):
    return pltpu.CompilerParams(dimension_semantics=semantics, vmem_limit_bytes=VMEM_LIMIT)


def _silu(x):
    return x * jax.nn.sigmoid(x)


def _norm_mod(x, g, shift, scale):
    ms = jnp.mean(x * x, axis=-1, keepdims=True)
    y = x * lax.rsqrt(ms + NORM_EPS) * g
    return y * (1.0 + scale) + shift


def _ada_kernel(c_ref, w_ref, b_ref, o_ref):
    s = _silu(c_ref[...])
    o_ref[...] = jnp.dot(s, w_ref[...], preferred_element_type=F32,
                         precision=lax.Precision.HIGHEST) + b_ref[...]


def _ada_call(cvec, w_ada, b_ada):
    depth, d, width = w_ada.shape
    rows = cvec.shape[0]
    tile = width // 4
    return pl.pallas_call(
        _ada_kernel,
        grid=(depth, width // tile),
        in_specs=[pl.BlockSpec((rows, d), lambda l, j: (0, 0)),
                  pl.BlockSpec((None, d, tile), lambda l, j: (l, 0, j)),
                  pl.BlockSpec((None, 1, tile), lambda l, j: (l, 0, j))],
        out_specs=pl.BlockSpec((None, rows, tile), lambda l, j: (l, 0, j)),
        out_shape=jax.ShapeDtypeStruct((depth, rows, width), F32),
        compiler_params=_params("arbitrary", "arbitrary"),
        name="ada",
    )(cvec, w_ada, b_ada)


PROJ_PIECES = (
    ("na_q", 0, 512, HEAD_DIM ** -0.5, BF16),
    ("na_k", 512, 512, 1.0, BF16),
    ("na_v", 1024, 512, 1.0, BF16),
    ("ret_q", 1536, 256, 1.0, F32),
    ("ret_k", 1792, 256, 1.0, F32),
    ("ret_v", 2048, 512, 1.0, BF16),
    ("ret_g", 2560, 512, 1.0, F32),
    ("pool", 3072, 512, 1.0, F32),
)
PROJ_WIDTH = 3584
RET_QK_START, RET_QK_END = 1536, 2048


def _proj_kernel(h_ref, mod_ref, g_ref, w_ref, *o_refs):
    hn = _norm_mod(h_ref[...], g_ref[...], mod_ref[0:1, :], mod_ref[1:2, :]).astype(BF16)
    for o_ref, (_, start, width, scale, _) in zip(o_refs, PROJ_PIECES):
        acc = jnp.dot(hn, w_ref[:, start:start + width], preferred_element_type=F32)
        if scale != 1.0:
            acc = acc * scale
        o_ref[...] = acc.astype(o_ref.dtype)


def _mod_spec(mod):
    d = mod.shape[-1]
    if mod.shape[0] == 1:
        return pl.BlockSpec((None, 8, d), lambda b, i: (0, 0, 0))
    return pl.BlockSpec((None, 8, d), lambda b, i: (b, 0, 0))


def _const_spec(shape):
    zeros = (0,) * len(shape)
    return pl.BlockSpec(shape, lambda b, i: zeros)


def _proj_call(h, mod, g, w, block):
    bsz, n, d = h.shape
    tok = lambda width: pl.BlockSpec((None, block, width), lambda b, i: (b, i, 0))
    return pl.pallas_call(
        _proj_kernel,
        grid=(bsz, n // block),
        in_specs=[tok(d), _mod_spec(mod), _const_spec((1, d)), _const_spec(w.shape)],
        out_specs=[tok(p[2]) for p in PROJ_PIECES],
        out_shape=[jax.ShapeDtypeStruct((bsz, n, p[2]), p[4]) for p in PROJ_PIECES],
        compiler_params=_params("arbitrary", "arbitrary"),
        name="proj",
    )(h, mod, g, w)


def _pair_softmax_pv(s_list, v_list):
    m = functools.reduce(jnp.maximum, [jnp.max(s, axis=1, keepdims=True) for s in s_list])
    p_list = [jnp.exp(s - m) for s in s_list]
    l = functools.reduce(jnp.add, [jnp.sum(p, axis=1, keepdims=True) for p in p_list])
    o = functools.reduce(jnp.add, [jnp.dot(p.astype(BF16), v, preferred_element_type=F32)
                                   for p, v in zip(p_list, v_list)])
    return o / l


def _stack_heads(q, first):
    zero = jnp.zeros_like(q)
    return jnp.concatenate([jnp.where(first, q, zero), jnp.where(first, zero, q)], axis=0)


def _na_kernel(q_ref, k_ref, v_ref, kc_ref, vc_ref, tab_ref, o_ref,
               vaug_ref, vcaug_ref, s_ref, p_ref, *, rows):
    first = lax.broadcasted_iota(jnp.int32, (GRID_W, LANES), 1) < HEAD_DIM
    win = NA_ROWS * GRID_W
    vaug_ref[:, :LANES] = v_ref[...]
    vaug_ref[:, LANES:] = jnp.ones((v_ref.shape[0], LANES), BF16)
    vcaug_ref[:, :LANES] = vc_ref[...]
    vcaug_ref[:, LANES:] = jnp.ones((vc_ref.shape[0], LANES), BF16)

    def window_start(r):
        return jnp.clip(r - NA_ROWS // 2, 0, rows - NA_ROWS)

    def scores(r, slot):
        r0 = window_start(r)
        delta = r - r0
        q2 = _stack_heads(q_ref[pl.ds(pl.multiple_of(r * GRID_W, GRID_W), GRID_W), :], first)
        kw = k_ref[pl.ds(pl.multiple_of(r0 * GRID_W, GRID_W), win), :]
        s_win = lax.dot_general(q2, kw, NT_DIMS, preferred_element_type=F32)
        for m in range(NA_ROWS // 2):
            cols = slice(m * LANES, (m + 1) * LANES)
            s_ref[slot, :, cols] = s_win[:, cols] + tab_ref[2 * m - delta + (NA_ROWS - 1)]
        s_ref[slot, :, win:] = lax.dot_general(q2, kc_ref[...], NT_DIMS, preferred_element_type=F32)

    def numerators(slot):
        s = s_ref[slot]
        p_ref[slot] = jnp.exp(s - jnp.max(s, axis=1, keepdims=True)).astype(BF16)

    def values(r, slot):
        r0 = window_start(r)
        vw = vaug_ref[pl.ds(pl.multiple_of(r0 * GRID_W, GRID_W), win), :]
        o = (jnp.dot(p_ref[slot, :, :win], vw, preferred_element_type=F32)
             + jnp.dot(p_ref[slot, :, win:], vcaug_ref[...], preferred_element_type=F32))
        o = o[:, :LANES] / o[:, LANES:]
        o_ref[pl.ds(pl.multiple_of(r * GRID_W, GRID_W), GRID_W), :] = jnp.where(
            first, o[:GRID_W], o[GRID_W:]).astype(o_ref.dtype)

    scores(0, 0)
    scores(1, 1)
    numerators(0)

    def steady(i, slot):
        scores(i + 2, slot)
        numerators(1 - slot)
        values(i, slot)

    def body(j, carry):
        for e in range(NA_ROWS_PER_TRIP):
            steady(NA_ROWS_PER_TRIP * j + e, e % 2)
        return carry

    trips = (rows - 2) // NA_ROWS_PER_TRIP
    lax.fori_loop(0, trips, body, 0)
    for i in range(trips * NA_ROWS_PER_TRIP, rows - 2):
        steady(i, i % 2)
    numerators((rows - 1) % 2)
    values(rows - 2, (rows - 2) % 2)
    values(rows - 1, (rows - 1) % 2)


def _na_bias_table(rpb):
    qc = np.arange(GRID_W)[:, None]
    kc = np.arange(GRID_W)[None, :]
    wstart = np.clip(qc - NA_COLS // 2, 0, GRID_W - NA_COLS)
    ok = (kc >= wstart) & (kc < wstart + NA_COLS)
    nh, nd = rpb.shape[0], rpb.shape[1]
    vec = rpb.astype(F32)
    period = jnp.concatenate([vec[..., NA_COLS - 1:],
                              jnp.zeros((nh, nd, LANES - (2 * NA_COLS - 1)), F32),
                              vec[..., :NA_COLS - 1]], axis=-1)
    flat = jnp.broadcast_to(period[:, :, None, :], (nh, nd, GRID_W, LANES)).reshape(nh, nd, -1)
    toeplitz = flat[..., :GRID_W * (LANES - 1)].reshape(nh, nd, GRID_W, LANES - 1)[..., :GRID_W]
    full = jnp.where(ok[None, None], toeplitz, NEG_INF)
    two = jnp.concatenate([full[:, :-1], full[:, 1:]], axis=-1)
    h, nd = two.shape[0], two.shape[1]
    return two.reshape(h // 2, 2, nd, GRID_W, LANES).transpose(0, 2, 1, 3, 4).reshape(
        h // 2, nd, 2 * GRID_W, LANES)


def _na_call(q, k, v, kc, vc, tab):
    bsz, n, width = q.shape
    nctx = kc.shape[1]
    pairs = width // LANES
    seq = lambda length: pl.BlockSpec((None, length, LANES), lambda b, p: (b, 0, p))
    return pl.pallas_call(
        functools.partial(_na_kernel, rows=n // GRID_W),
        grid=(bsz, pairs),
        in_specs=[seq(n), seq(n), seq(n), seq(nctx), seq(nctx),
                  pl.BlockSpec((None,) + tab.shape[1:], lambda b, p: (p, 0, 0, 0))],
        out_specs=seq(n),
        out_shape=jax.ShapeDtypeStruct((bsz, n, width), BF16),
        scratch_shapes=[pltpu.VMEM((n, 2 * LANES), BF16),
                        pltpu.VMEM((nctx, 2 * LANES), BF16),
                        pltpu.VMEM((2, 2 * GRID_W, NA_ROWS * GRID_W + nctx), F32),
                        pltpu.VMEM((2, 2 * GRID_W, NA_ROWS * GRID_W + nctx), BF16)],
        compiler_params=_params("arbitrary", "arbitrary"),
        name="na",
    )(q, k, v, kc, vc, tab)


def _ctx_attn_kernel(q_ref, k_ref, v_ref, o_ref):
    n = q_ref.shape[0]
    first = lax.broadcasted_iota(jnp.int32, (n, LANES), 1) < HEAD_DIM
    q2 = _stack_heads(q_ref[...], first)
    s = lax.dot_general(q2, k_ref[...], NT_DIMS, preferred_element_type=F32)
    o = _pair_softmax_pv([s], [v_ref[...]])
    o_ref[...] = jnp.where(first, o[:n], o[n:]).astype(o_ref.dtype)


def _ctx_attn_call(q, k, v):
    bsz, n, width = q.shape
    seq = pl.BlockSpec((None, n, LANES), lambda b, p: (b, 0, p))
    return pl.pallas_call(
        _ctx_attn_kernel,
        grid=(bsz, width // LANES),
        in_specs=[seq, seq, seq],
        out_specs=seq,
        out_shape=jax.ShapeDtypeStruct((bsz, n, width), BF16),
        compiler_params=_params("arbitrary", "arbitrary"),
        name="ctx_attn",
    )(q, k, v)


C = RET_CHUNK
RET_PAIRS = RET_HEADS // 2
T_XIF, T_XIB, T_ZF, T_ZB = 0, 1, 2, 3
W_DEC, W_CF, W_CB = 0, 1, 2


def _ret_tables(lg_ref, tabs_ref, wide_ref):
    row = lax.broadcasted_iota(jnp.int32, (C, C), 0)
    col = lax.broadcasted_iota(jnp.int32, (C, C), 1)
    rowf = row.astype(F32)
    diff = (row - col).astype(F32)

    def both_halves(x):
        return jnp.concatenate([x, x], axis=1)

    for p in range(RET_PAIRS):
        lf = (lg_ref[0, 2 * p], lg_ref[0, 2 * p + 1])
        lb = (lg_ref[1, 2 * p], lg_ref[1, 2 * p + 1])
        lf_lane = jnp.where(_head0(col), lf[0], lf[1])
        lb_lane = jnp.where(_head0(col), lb[0], lb[1])
        lf_row = jnp.where(_head0(row), lf[0], lf[1])
        lb_row = jnp.where(_head0(row), lb[0], lb[1])
        tabs_ref[T_XIF * RET_PAIRS + p] = jnp.exp(lf_lane * (rowf + 1.0))
        tabs_ref[T_XIB * RET_PAIRS + p] = jnp.exp(lb_lane * (C - rowf))
        tabs_ref[T_ZF * RET_PAIRS + p] = jnp.exp(lf_lane * (C - 1.0 - rowf))
        tabs_ref[T_ZB * RET_PAIRS + p] = jnp.exp(lb_lane * rowf)
        wide_ref[W_DEC * RET_PAIRS + p] = jnp.concatenate(
            [jnp.where(diff >= 0, jnp.exp(lf[j] * jnp.maximum(diff, 0.0)),
                       jnp.exp(lb[j] * jnp.maximum(-diff, 0.0))) for j in range(2)], axis=1)
        wide_ref[W_CF * RET_PAIRS + p] = both_halves(jnp.exp(lf_row * float(C)))
        wide_ref[W_CB * RET_PAIRS + p] = both_halves(jnp.exp(lb_row * float(C)))


def _head0(index):
    return (index % RET_DK) < RET_DK // 2


def _rope(x, cos, sin):
    return x * cos + pltpu.roll(x, LANES // 2, 1) * sin


def _rope_tables(n):
    nf = RET_DK // 4
    inv = ROPE_BASE ** (-np.arange(nf, dtype=np.float64) / nf)
    t = np.arange(n)
    ang_row = (t // GRID_W).astype(np.float64)[:, None] * inv[None, :]
    ang_col = (t % GRID_W).astype(np.float64)[:, None] * inv[None, :]
    half = np.concatenate([ang_row, ang_col, ang_row, ang_col], axis=1)
    cos = np.concatenate([np.cos(half), np.cos(half)], axis=1)
    sin = np.concatenate([-np.sin(half), np.sin(half)], axis=1)
    return jnp.asarray(cos, F32), jnp.asarray(sin, F32)


def _ret_qk_layout(w):
    lead = w.shape[:-1]
    nl = len(lead)
    w = w.reshape(lead + (2, RET_PAIRS, 2, 2, 2, RET_DK // 4))
    w = w.transpose(tuple(range(nl)) + (nl, nl + 1, nl + 4, nl + 2, nl + 3, nl + 5))
    return w.reshape(lead + (2 * RET_HEADS * RET_DK,))


def _state_delta(kz, v2, diagonal):
    full = lax.dot_general(kz.astype(BF16), v2, TN_DIMS, preferred_element_type=F32)
    return jnp.where(diagonal, full, 0.0)


def _ret_kernel(lg_ref, qx_ref, kx_ref, vx_ref, gx_ref, qc_ref, kc_ref, vc_ref, gc_ref,
                cos_ref, sin_ref, gn_ref, ox_ref, oc_ref,
                tabs_ref, wide_ref, sbs_ref, sb_ref, sf_ref, *, ctx_chunks, block_chunks, x_blocks):
    phase = pl.program_id(1)
    t = pl.program_id(2)
    blk = jnp.where(phase == 0, x_blocks - t, t - 1)
    u_latent = ctx_chunks + blk * block_chunks
    lane_first = _head0(lax.broadcasted_iota(jnp.int32, (C, LANES), 1))
    wide_row = lax.broadcasted_iota(jnp.int32, (C, 2 * RET_DV), 0)
    wide_col = lax.broadcasted_iota(jnp.int32, (C, 2 * RET_DV), 1)
    col_first = wide_col < RET_DV
    diagonal = _head0(wide_row) == col_first
    ksc = RET_DK ** -0.5
    pairs = range(RET_PAIRS)

    @pl.when((phase == 0) & (t == 0))
    def _():
        _ret_tables(lg_ref, tabs_ref, wide_ref)
        sb_ref[...] = jnp.zeros_like(sb_ref)
        sf_ref[...] = jnp.zeros_like(sf_ref)

    def load_qk(ref, rows, p, rope_off):
        x = ref[rows, p * LANES:(p + 1) * LANES]
        if rope_off is not None:
            off = pl.multiple_of(rope_off + rows.start, C)
            x = _rope(x, cos_ref[pl.ds(off, C), :], sin_ref[pl.ds(off, C), :])
        return x

    def pair_values(v_ref, rows, p):
        return v_ref[rows, 2 * p * RET_DV:(2 * p + 2) * RET_DV]

    def backward_block(k_ref, v_ref, nchunks, u0, rope_off):
        states = [sb_ref[p] for p in pairs]
        for c in reversed(range(nchunks)):
            rows = slice(c * C, (c + 1) * C)
            for p in pairs:
                sbs_ref[u0 + c, p] = states[p].astype(BF16)
                ks = load_qk(k_ref, rows, p, rope_off) * ksc
                delta = _state_delta(ks * tabs_ref[T_ZB * RET_PAIRS + p], pair_values(v_ref, rows, p),
                                     diagonal)
                states[p] = states[p] * wide_ref[W_CB * RET_PAIRS + p] + delta
        for p in pairs:
            sb_ref[p] = states[p]

    def forward_block(q_ref, k_ref, v_ref, g_ref, o_ref, nchunks, u0, rope_off):
        states = [sf_ref[p] for p in pairs]
        items = [(c, p) for c in range(nchunks) for p in pairs]
        live = {}

        def scores_and_state(c, p):
            rows = slice(c * C, (c + 1) * C)
            q = load_qk(q_ref, rows, p, rope_off)
            ks = load_qk(k_ref, rows, p, rope_off) * ksc
            v2 = pair_values(v_ref, rows, p)
            zero_k = jnp.zeros_like(ks)
            k2 = jnp.concatenate([jnp.where(lane_first, ks, zero_k).astype(BF16),
                                  jnp.where(lane_first, zero_k, ks).astype(BF16)], axis=0)
            s2 = lax.dot_general(q.astype(BF16), k2, NT_DIMS, preferred_element_type=F32)
            live[c, p] = dict(rows=rows, q=q, v2=v2, s2=s2, state=states[p])
            delta = _state_delta(ks * tabs_ref[T_ZF * RET_PAIRS + p], v2, diagonal)
            states[p] = states[p] * wide_ref[W_CF * RET_PAIRS + p] + delta

        def outputs(c, p):
            it = live[c, p]
            q, v2 = it["q"], it["v2"]
            zero_v = jnp.zeros_like(v2)
            lhs = jnp.concatenate(
                [(it["s2"] * wide_ref[W_DEC * RET_PAIRS + p]).astype(BF16),
                 (q * tabs_ref[T_XIF * RET_PAIRS + p]).astype(BF16),
                 (q * tabs_ref[T_XIB * RET_PAIRS + p]).astype(BF16)], axis=1)
            rhs = jnp.concatenate(
                [jnp.where(col_first, v2, zero_v), jnp.where(col_first, zero_v, v2),
                 it["state"].astype(BF16), sbs_ref[u0 + c, p]], axis=0)
            live[c, p] = dict(rows=it["rows"], o2=jnp.dot(lhs, rhs, preferred_element_type=F32))

        def normalise(c, p):
            it = live.pop((c, p))
            rows = it["rows"]
            for j in range(2):
                cols = slice((2 * p + j) * RET_DV, (2 * p + j + 1) * RET_DV)
                o = it["o2"][:, j * RET_DV:(j + 1) * RET_DV]
                mu = jnp.mean(o, axis=-1, keepdims=True)
                d = o - mu
                var = jnp.mean(d * d, axis=-1, keepdims=True)
                y = d * lax.rsqrt(var + GN_EPS) * gn_ref[:, cols]
                o_ref[rows, cols] = (_silu(g_ref[rows, cols]) * y).astype(o_ref.dtype)

        stages = (scores_and_state, outputs, normalise)
        for step in range(len(items) + len(stages) - 1):
            for depth, stage in enumerate(stages):
                if 0 <= step - depth < len(items):
                    stage(*items[step - depth])
        for p in pairs:
            sf_ref[p] = states[p]

    rope_off = blk * (block_chunks * C)

    @pl.when((phase == 0) & (t == 0))
    def _():
        backward_block(kc_ref, vc_ref, ctx_chunks, 0, None)

    @pl.when((phase == 0) & (t > 0))
    def _():
        backward_block(kx_ref, vx_ref, block_chunks, u_latent, rope_off)

    @pl.when((phase == 1) & (t == 0))
    def _():
        forward_block(qc_ref, kc_ref, vc_ref, gc_ref, oc_ref, ctx_chunks, 0, None)

    @pl.when((phase == 1) & (t > 0))
    def _():
        forward_block(qx_ref, kx_ref, vx_ref, gx_ref, ox_ref, block_chunks, u_latent, rope_off)


RET_BLOCK_CHUNKS = 4


def _ret_call(lg, qx, kx, vx, gx, qc, kc, vc, gc, cos, sin, gn):
    bsz, n, _ = qx.shape
    nctx = qc.shape[1]
    ctx_chunks, x_chunks = nctx // C, n // C
    block_chunks = RET_BLOCK_CHUNKS
    x_blocks = x_chunks // block_chunks
    assert x_blocks * block_chunks == x_chunks

    def last_block(i):
        return jnp.clip(i, 0, x_blocks - 1)

    def both(width):
        return pl.BlockSpec((None, block_chunks * C, width),
                            lambda b, ph, t: (b, last_block(jnp.where(ph == 0, x_blocks - t, t - 1)), 0))

    def fwd_only(width):
        return pl.BlockSpec((None, block_chunks * C, width),
                            lambda b, ph, t: (b, last_block(ph * (t - 1)), 0))

    def ctx_all(width):
        return pl.BlockSpec((None, nctx, width), lambda b, ph, t: (b, 0, 0))

    qk_w, v_w = qx.shape[2], vx.shape[2]
    return pl.pallas_call(
        functools.partial(_ret_kernel, ctx_chunks=ctx_chunks, block_chunks=block_chunks,
                          x_blocks=x_blocks),
        grid=(bsz, 2, 1 + x_blocks),
        in_specs=[pl.BlockSpec(memory_space=pltpu.SMEM),
                  fwd_only(qk_w), both(qk_w), both(v_w), fwd_only(v_w),
                  ctx_all(qk_w), ctx_all(qk_w), ctx_all(v_w), ctx_all(v_w),
                  pl.BlockSpec(cos.shape, lambda b, ph, t: (0, 0)),
                  pl.BlockSpec(sin.shape, lambda b, ph, t: (0, 0)),
                  pl.BlockSpec(gn.shape, lambda b, ph, t: (0, 0))],
        out_specs=[fwd_only(v_w), ctx_all(v_w)],
        out_shape=[jax.ShapeDtypeStruct((bsz, n, v_w), BF16),
                   jax.ShapeDtypeStruct((bsz, nctx, v_w), BF16)],
        scratch_shapes=[pltpu.VMEM((4 * RET_PAIRS, C, C), F32),
                        pltpu.VMEM((3 * RET_PAIRS, C, 2 * C), F32),
                        pltpu.VMEM((ctx_chunks + x_chunks, RET_PAIRS, 2 * RET_DK, 2 * RET_DV), BF16),
                        pltpu.VMEM((RET_PAIRS, 2 * RET_DK, 2 * RET_DV), F32),
                        pltpu.VMEM((RET_PAIRS, 2 * RET_DK, 2 * RET_DV), F32)],
        compiler_params=_params("arbitrary", "arbitrary", "arbitrary"),
        name="retention",
    )(lg, qx, kx, vx, gx, qc, kc, vc, gc, cos, sin, gn)


def _merge_kernel(h_ref, mod_ref, g_ref, a_ref, r_ref, u_ref, uprev_ref, unext_ref,
                  wg_ref, wpa_ref, wpb_ref, wpc_ref, wo_ref, pw_ref, ps_ref, o_ref, ext_ref, *, seq_len):
    i = pl.program_id(1)
    block = h_ref.shape[0]
    d = h_ref.shape[1]

    zero_halo = jnp.zeros((POOL_HALO, u_ref.shape[1]), F32)
    ext_ref[0:POOL_HALO, :] = jnp.where(i > 0, uprev_ref[...], zero_halo)
    ext_ref[POOL_HALO:POOL_HALO + block, :] = u_ref[...]
    ext_ref[POOL_HALO + block:, :] = jnp.where(i < pl.num_programs(1) - 1, unext_ref[...], zero_halo)
    edge_row = lax.broadcasted_iota(jnp.int32, (POOL_HALO, LANES), 0)

    def window_mean(acc, w, first_pos):
        def edge(part, pos0):
            tpos = pos0 + edge_row
            cnt = jnp.minimum(tpos + w // 2, seq_len) - jnp.maximum(tpos - w // 2, 0)
            return part / cnt.astype(F32)
        n_rows = acc.shape[0]
        return jnp.concatenate([edge(acc[:POOL_HALO], first_pos),
                                acc[POOL_HALO:n_rows - POOL_HALO] * (1.0 / w),
                                edge(acc[n_rows - POOL_HALO:], first_pos + n_rows - POOL_HALO)], axis=0)

    for r0 in range(0, block, SUB_ROWS):
        sub = min(SUB_ROWS, block - r0)
        rows = slice(r0, r0 + sub)
        h = h_ref[rows, :]
        hn = _norm_mod(h, g_ref[...], mod_ref[0:1, :], mod_ref[1:2, :]).astype(BF16)
        pooled_out = []
        for gi, w in enumerate(POOL_WINDOWS):
            lanes = slice(gi * LANES, (gi + 1) * LANES)
            base = POOL_HALO + r0
            acc = ext_ref[base - w // 2:base - w // 2 + sub, lanes]
            for s in range(1 - w // 2, w // 2):
                acc = acc + ext_ref[base + s:base + s + sub, lanes]
            pooled = window_mean(acc, w, i * block + r0) - u_ref[rows, lanes]
            pooled_out.append(jnp.dot(pooled.astype(BF16), pw_ref[gi], preferred_element_type=F32))
        pool = (jnp.concatenate(pooled_out, axis=1) * ps_ref[...]).astype(BF16)

        def gated(k, x, w_ref):
            gate = jnp.dot(hn, wg_ref[:, k * d:(k + 1) * d], preferred_element_type=F32)
            return jax.nn.sigmoid(gate) * jnp.dot(x, w_ref[...], preferred_element_type=F32)

        m = (gated(0, a_ref[rows, :], wpa_ref) + gated(1, r_ref[rows, :], wpb_ref)
             + gated(2, pool, wpc_ref))
        mix = jnp.dot(m.astype(BF16), wo_ref[...], preferred_element_type=F32)
        o_ref[rows, :] = h + mod_ref[2:3, :] * mix


def _merge_call(h, mod, g, a, r, u, wg, wpa, wpb, wpc, wo, pw, ps, block):
    bsz, n, d = h.shape
    pw_w = u.shape[2]
    halo_blocks = block // POOL_HALO
    tok = lambda width: pl.BlockSpec((None, block, width), lambda b, i: (b, i, 0))
    prev = pl.BlockSpec((None, POOL_HALO, pw_w),
                        lambda b, i: (b, jnp.maximum(i * halo_blocks - 1, 0), 0))
    nxt = pl.BlockSpec((None, POOL_HALO, pw_w),
                       lambda b, i: (b, jnp.minimum((i + 1) * halo_blocks, n // POOL_HALO - 1), 0))
    return pl.pallas_call(
        functools.partial(_merge_kernel, seq_len=n),
        grid=(bsz, n // block),
        in_specs=[tok(d), _mod_spec(mod), _const_spec((1, d)), tok(a.shape[2]), tok(r.shape[2]),
                  tok(pw_w), prev, nxt,
                  _const_spec(wg.shape), _const_spec(wpa.shape), _const_spec(wpb.shape),
                  _const_spec(wpc.shape), _const_spec(wo.shape), _const_spec(pw.shape),
                  _const_spec(ps.shape)],
        out_specs=tok(d),
        out_shape=jax.ShapeDtypeStruct((bsz, n, d), F32),
        scratch_shapes=[pltpu.VMEM((block + 2 * POOL_HALO, pw_w), F32)],
        compiler_params=_params("arbitrary", "arbitrary"),
        name="merge",
    )(h, mod, g, a, r, u, u, u, wg, wpa, wpb, wpc, wo, pw, ps)


def _ffn_kernel(h_ref, mod_ref, g_ref, wg_ref, wu_ref, wd_ref, fg_ref, o_ref, *, hidden_tile, final):
    hidden = wg_ref.shape[1]
    for r0 in range(0, h_ref.shape[0], SUB_ROWS):
        rows = slice(r0, min(r0 + SUB_ROWS, h_ref.shape[0]))
        h = h_ref[rows, :]
        f = _norm_mod(h, g_ref[...], mod_ref[3:4, :], mod_ref[4:5, :]).astype(BF16)
        acc = None
        for start in range(0, hidden, hidden_tile):
            cols = slice(start, start + hidden_tile)
            gate = jnp.dot(f, wg_ref[:, cols], preferred_element_type=F32)
            up = jnp.dot(f, wu_ref[:, cols], preferred_element_type=F32)
            part = jnp.dot((_silu(gate) * up).astype(BF16), wd_ref[cols, :],
                           preferred_element_type=F32)
            acc = part if acc is None else acc + part
        out = h + mod_ref[5:6, :] * acc
        if final:
            ms = jnp.mean(out * out, axis=-1, keepdims=True)
            out = out * lax.rsqrt(ms + NORM_EPS) * fg_ref[...]
        o_ref[rows, :] = out


def _ffn_call(h, mod, g, wg, wu, wd, final_g, block, final):
    bsz, n, d = h.shape
    hidden = wg.shape[1]
    tok = pl.BlockSpec((None, block, d), lambda b, i: (b, i, 0))
    return pl.pallas_call(
        functools.partial(_ffn_kernel, hidden_tile=hidden, final=final),
        grid=(bsz, n // block),
        in_specs=[tok, _mod_spec(mod), _const_spec((1, d)), _const_spec(wg.shape),
                  _const_spec(wu.shape), _const_spec(wd.shape), _const_spec((1, d))],
        out_specs=tok,
        out_shape=jax.ShapeDtypeStruct((bsz, n, d), F32),
        compiler_params=_params("arbitrary", "arbitrary"),
        name="ffn",
    )(h, mod, g, wg, wu, wd, final_g)


X_BLOCK = 1024
CTX_BLOCK = 256


def kernel(x, c, ctx, c_ctx, norm1_g, norm2_g, w_ada, b_ada, w_in, na_rpb, ret_logit_f, ret_logit_b,
           ret_gn_g, pool_w, pool_scale, w_branch_a, w_branch_b, w_branch_c, w_out, w_ffn_gate,
           w_ffn_up, w_ffn_down, final_norm_g):
    bsz, n, d = x.shape
    depth = w_in.shape[0]
    assert n % GRID_W == 0 and n // GRID_W >= NA_ROWS and n % X_BLOCK == 0
    assert ctx.shape[1] == CTX_BLOCK and bsz <= 8

    cvec = jnp.zeros((16, d), F32).at[:bsz].set(c).at[8].set(c_ctx)
    ada = _ada_call(cvec, w_ada, b_ada.reshape(depth, 1, -1)).reshape(depth, 16, N_MOD, d)
    ada = jnp.pad(ada, ((0, 0), (0, 0), (0, 8 - N_MOD), (0, 0)))

    w_in_b = w_in.astype(BF16)
    wpa_b, wpb_b, wpc_b = (w.astype(BF16) for w in (w_branch_a, w_branch_b, w_branch_c))
    wo_b, pw_b = w_out.astype(BF16), pool_w.astype(BF16)
    wfg_b, wfu_b, wfd_b = (w.astype(BF16) for w in (w_ffn_gate, w_ffn_up, w_ffn_down))
    lg = jnp.stack([jax.nn.log_sigmoid(ret_logit_f.astype(F32)),
                    jax.nn.log_sigmoid(ret_logit_b.astype(F32))], axis=1)
    cos, sin = _rope_tables(n)
    row = lambda v: v.reshape(1, -1)

    h, hc = x, ctx
    for l in range(depth):
        last = l == depth - 1
        mod_x, mod_c = ada[l, :bsz], ada[l, 8:9]
        w_gate = w_in_b[l, :, PROJ_WIDTH:]
        w_proj = jnp.concatenate([w_in_b[l, :, :RET_QK_START],
                                  _ret_qk_layout(w_in_b[l, :, RET_QK_START:RET_QK_END]),
                                  w_in_b[l, :, RET_QK_END:PROJ_WIDTH]], axis=1)
        ux = dict(zip([p[0] for p in PROJ_PIECES],
                      _proj_call(h, mod_x, row(norm1_g[l]), w_proj, X_BLOCK)))
        uc = dict(zip([p[0] for p in PROJ_PIECES],
                      _proj_call(hc, mod_c, row(norm1_g[l]), w_proj, CTX_BLOCK)))
        a_x = _na_call(ux["na_q"], ux["na_k"], ux["na_v"], uc["na_k"], uc["na_v"],
                       _na_bias_table(na_rpb[l]))
        r_x, r_c = _ret_call(lg[l], ux["ret_q"], ux["ret_k"], ux["ret_v"], ux["ret_g"],
                             uc["ret_q"], uc["ret_k"], uc["ret_v"], uc["ret_g"],
                             cos, sin, row(ret_gn_g[l]))
        merge_w = (w_gate, wpa_b[l], wpb_b[l], wpc_b[l], wo_b[l], pw_b[l], row(pool_scale[l]))
        ffn_w = (wfg_b[l], wfu_b[l], wfd_b[l], row(final_norm_g))
        h = _merge_call(h, mod_x, row(norm1_g[l]), a_x, r_x, ux["pool"], *merge_w, X_BLOCK)
        h = _ffn_call(h, mod_x, row(norm2_g[l]), *ffn_w, X_BLOCK, last)
        if not last:
            a_c = _ctx_attn_call(uc["na_q"], uc["na_k"], uc["na_v"])
            hc = _merge_call(hc, mod_c, row(norm1_g[l]), a_c, r_c, uc["pool"], *merge_w, CTX_BLOCK)
            hc = _ffn_call(hc, mod_c, row(norm2_g[l]), *ffn_w, CTX_BLOCK, False)
    return h
```
